```python
import math
import jax, jax.numpy as jnp
from jax import lax
import numpy as np

D_MODEL = 4096
BATCH = 4
SEQ = 4096
DEPTH = 4

N_MEM = 256
EPS = 1e-6
BLOCK = 128
MIX_WIDTH = D_MODEL
X_HEADS = 4
X_HEAD_DIM = MIX_WIDTH // 4 // X_HEADS
X_WIDTH = X_HEADS * X_HEAD_DIM
SELF_WIDTH = MIX_WIDTH - X_WIDTH
A_NOPE = 128
A_ROPE = 64
A_VDIM = 128
A_HEADS = SELF_WIDTH // A_VDIM
A_Q_RANK = 1024
A_KV_RANK = 512
ROPE_THETA = 10000.0
B_HEAD_DIM = 64
B_HEADS = SELF_WIDTH // B_HEAD_DIM
B_KV_HEADS = 8
B_GROUP = B_HEADS // B_KV_HEADS
WINDOW = 128
N_BUCKETS = 32
MAX_EXACT = N_BUCKETS // 2
MAX_DIST = WINDOW
N_MIXERS = 2
N_A = (DEPTH + 1) // 2
N_B = DEPTH // 2
A_IN = A_Q_RANK + A_KV_RANK + A_ROPE + X_WIDTH + MIX_WIDTH
B_IN = B_HEADS * B_HEAD_DIM + 2 * B_KV_HEADS * B_HEAD_DIM + X_WIDTH + MIX_WIDTH

kernel_name = "hybrid_mla_swa_sink_memxattn_gated"


def rmsnorm(x, g):
    xf = x.astype(jnp.float32)
    y = xf * lax.rsqrt(jnp.mean(xf * xf, axis=-1, keepdims=True) + EPS)
    return (y * g.astype(jnp.float32)).astype(x.dtype)


def split_cols(t, sizes):
    outs, off = [], 0
    for s in sizes:
        outs.append(t[..., off:off + s])
        off += s
    return outs


def rope_tables(positions):
    inv = ROPE_THETA ** (-jnp.arange(0, A_ROPE, 2, dtype=jnp.float32) / A_ROPE)
    ang = positions.astype(jnp.float32)[..., None] * inv
    return jnp.cos(ang), jnp.sin(ang)


def apply_rope(t, cos, sin):
    tf = t.astype(jnp.float32)
    t1, t2 = tf[..., :A_ROPE // 2], tf[..., A_ROPE // 2:]
    return jnp.concatenate([t1 * cos - t2 * sin, t2 * cos + t1 * sin], axis=-1).astype(t.dtype)


def t5_bucket(dist):
    n = jnp.maximum(dist, 0)
    nf = jnp.maximum(n, 1).astype(jnp.float32)
    large = MAX_EXACT + (jnp.log(nf / MAX_EXACT) / math.log(MAX_DIST / MAX_EXACT)
                         * (N_BUCKETS - MAX_EXACT)).astype(jnp.int32)
    large = jnp.minimum(large, N_BUCKETS - 1)
    return jnp.where(n < MAX_EXACT, n, large)


def memory_attention(q, mem_k, mem_v):
    B, S = q.shape[:2]
    s = jnp.einsum('bshd,bmhd->bhsm', q, mem_k).astype(jnp.float32) * (X_HEAD_DIM ** -0.5)
    p = jax.nn.softmax(s, axis=-1).astype(mem_v.dtype)
    return jnp.einsum('bhsm,bmhd->bshd', p, mem_v).reshape(B, S, X_WIDTH)


def mla_attention(q_nope, q_rope, k_nope, k_rope, v):
    B, S = q_nope.shape[:2]
    nblk = S // BLOCK
    scale = (A_NOPE + A_ROPE) ** -0.5
    key_idx = jnp.arange(S)
    q_local = jnp.arange(BLOCK)

    def to_blocks(t):
        return t.reshape(B, nblk, BLOCK, *t.shape[2:]).swapaxes(0, 1)

    def one_block(args):
        qn, qr, blk = args
        s = (jnp.einsum('bqhd,bkhd->bhqk', qn, k_nope)
             + jnp.einsum('bqhr,bkr->bhqk', qr, k_rope)).astype(jnp.float32) * scale
        causal = key_idx[None, :] <= (blk * BLOCK + q_local)[:, None]
        s = jnp.where(causal, s, -jnp.inf)
        p = jax.nn.softmax(s, axis=-1).astype(v.dtype)
        return jnp.einsum('bhqk,bkhd->bqhd', p, v)

    out = lax.map(one_block, (to_blocks(q_nope), to_blocks(q_rope), jnp.arange(nblk)))
    return out.swapaxes(0, 1).reshape(B, S, A_HEADS * A_VDIM)


def swa_attention(q, k, v, sinks, rel_bias):
    B, S = q.shape[:2]
    nblk = S // BLOCK

    def kv_bands(t):
        tp = jnp.pad(t, ((0, 0), (BLOCK, 0), (0, 0), (0, 0)))
        blocks = tp.reshape(B, nblk + 1, BLOCK, B_KV_HEADS, B_HEAD_DIM)
        return jnp.concatenate([blocks[:, :-1], blocks[:, 1:]], axis=2).swapaxes(0, 1)

    qb = q.reshape(B, nblk, BLOCK, B_KV_HEADS, B_GROUP, B_HEAD_DIM).swapaxes(0, 1)
    q_local = jnp.arange(BLOCK)[:, None]
    k_local = jnp.arange(2 * BLOCK)[None, :]
    dist = q_local + BLOCK - k_local
    in_window = (dist >= 0) & (dist < WINDOW)
    bias = rel_bias.astype(jnp.float32)[t5_bucket(dist)]
    bias = bias.transpose(2, 0, 1).reshape(B_KV_HEADS, B_GROUP, BLOCK, 2 * BLOCK)
    sink = sinks.astype(jnp.float32).reshape(B_KV_HEADS, B_GROUP)[None, :, :, None]
    scale = B_HEAD_DIM ** -0.5

    def one_block(args):
        qblk, kblk, vblk, blk = args
        s = jnp.einsum('bqhgd,bkhd->bhgqk', qblk, kblk).astype(jnp.float32) * scale + bias
        valid = in_window & (blk * BLOCK - BLOCK + k_local >= 0)
        s = jnp.where(valid, s, -jnp.inf)
        m = jnp.maximum(jnp.max(s, axis=-1), sink)
        p = jnp.exp(s - m[..., None])
        denom = jnp.sum(p, axis=-1) + jnp.exp(sink - m)
        p = (p / denom[..., None]).astype(vblk.dtype)
        return jnp.einsum('bhgqk,bkhd->bqhgd', p, vblk)

    out = lax.map(one_block, (qb, kv_bands(k), kv_bands(v), jnp.arange(nblk)))
    return out.swapaxes(0, 1).reshape(B, S, B_HEADS * B_HEAD_DIM)


def mla_mixer(h, cos, sin, w_in, q_norm_g, kv_norm_g, w_qb, w_kvb):
    B, S, _ = h.shape
    proj = jnp.einsum('bsd,de->bse', h, w_in)
    c_q, c_kv, k_rope, xq, z = split_cols(proj, [A_Q_RANK, A_KV_RANK, A_ROPE, X_WIDTH, MIX_WIDTH])
    q = jnp.einsum('bsr,re->bse', rmsnorm(c_q, q_norm_g), w_qb).reshape(B, S, A_HEADS, A_NOPE + A_ROPE)
    q_nope = q[..., :A_NOPE]
    q_rope = apply_rope(q[..., A_NOPE:], cos[:, :, None, :], sin[:, :, None, :])
    kv = jnp.einsum('bsr,re->bse', rmsnorm(c_kv, kv_norm_g), w_kvb).reshape(B, S, A_HEADS, A_NOPE + A_VDIM)
    k_nope, v = kv[..., :A_NOPE], kv[..., A_NOPE:]
    k_rope = apply_rope(k_rope, cos, sin)
    out = mla_attention(q_nope, q_rope, k_nope, k_rope, v)
    return out, xq.reshape(B, S, X_HEADS, X_HEAD_DIM), z


def swa_mixer(h, w_in, sinks, rel_bias):
    B, S, _ = h.shape
    proj = jnp.einsum('bsd,de->bse', h, w_in)
    q, k, v, xq, z = split_cols(proj, [B_HEADS * B_HEAD_DIM, B_KV_HEADS * B_HEAD_DIM,
                                       B_KV_HEADS * B_HEAD_DIM, X_WIDTH, MIX_WIDTH])
    q = q.reshape(B, S, B_HEADS, B_HEAD_DIM)
    k = k.reshape(B, S, B_KV_HEADS, B_HEAD_DIM)
    v = v.reshape(B, S, B_KV_HEADS, B_HEAD_DIM)
    out = swa_attention(q, k, v, sinks, rel_bias)
    return out, xq.reshape(B, S, X_HEADS, X_HEAD_DIM), z


def setup_inputs(seed: int = 0) -> dict:
    key = jax.random.key(seed)
    ks = jax.random.split(key, 20)
    f32 = jnp.float32

    def w(k, shape, fan_in):
        return jax.random.normal(k, shape, f32) * (fan_in ** -0.5)

    def gain(k, shape):
        return 1.0 + 0.02 * jax.random.normal(k, shape, f32)

    x = jax.random.normal(ks[0], (BATCH, SEQ, D_MODEL), f32)
    mem = jax.random.normal(ks[1], (BATCH, N_MEM, D_MODEL), f32)
    offset = jax.random.randint(ks[2], (BATCH, 1), 0, 1024, dtype=jnp.int32)
    positions = (offset + jnp.arange(SEQ, dtype=jnp.int32)[None, :]).astype(jnp.int32)
    return {
        "x": x,
        "mem": mem,
        "positions": positions,
        "norm_g": gain(ks[3], (DEPTH, D_MODEL)),
        "mem_norm_g": gain(ks[4], (DEPTH, D_MODEL)),
        "final_norm_g": gain(ks[5], (D_MODEL,)),
        "w_mem_kv": w(ks[6], (DEPTH, D_MODEL, 2 * X_WIDTH), D_MODEL),
        "w_out": w(ks[7], (DEPTH, MIX_WIDTH, D_MODEL), MIX_WIDTH),
        "a_w_in": w(ks[8], (N_A, D_MODEL, A_IN), D_MODEL),
        "a_q_norm_g": gain(ks[9], (N_A, A_Q_RANK)),
        "a_kv_norm_g": gain(ks[10], (N_A, A_KV_RANK)),
        "a_w_qb": w(ks[11], (N_A, A_Q_RANK, A_HEADS * (A_NOPE + A_ROPE)), A_Q_RANK),
        "a_w_kvb": w(ks[12], (N_A, A_KV_RANK, A_HEADS * (A_NOPE + A_VDIM)), A_KV_RANK),
        "b_w_in": w(ks[13], (N_B, D_MODEL, B_IN), D_MODEL),
        "b_sinks": 0.5 * jax.random.normal(ks[14], (N_B, B_HEADS), f32),
        "rel_bias": 0.5 * jax.random.normal(ks[15], (N_BUCKETS, B_HEADS), f32),
    }


def reference(x, mem, positions, norm_g, mem_norm_g, final_norm_g, w_mem_kv, w_out,
              a_w_in, a_q_norm_g, a_kv_norm_g, a_w_qb, a_w_kvb, b_w_in, b_sinks, rel_bias):
    B = x.shape[0]
    cos, sin = rope_tables(positions)
    for i in range(DEPTH):
        h = rmsnorm(x, norm_g[i])
        mn = rmsnorm(mem, mem_norm_g[i])
        mkv = jnp.einsum('bmd,de->bme', mn, w_mem_kv[i]).reshape(B, N_MEM, 2, X_HEADS, X_HEAD_DIM)
        j = i // N_MIXERS
        if i % N_MIXERS == 0:
            self_out, xq, z = mla_mixer(h, cos, sin, a_w_in[j], a_q_norm_g[j], a_kv_norm_g[j],
                                        a_w_qb[j], a_w_kvb[j])
        else:
            self_out, xq, z = swa_mixer(h, b_w_in[j], b_sinks[j], rel_bias)
        mem_out = memory_attention(xq, mkv[:, :, 0], mkv[:, :, 1])
        y = jnp.concatenate([self_out, mem_out], axis=-1) * jax.nn.silu(z)
        x = x + jnp.einsum('bse,ed->bsd', y, w_out[i])
    return rmsnorm(x, final_norm_g)
```

```python
import functools
import math

import jax
import jax.numpy as jnp
from jax import lax
from jax.experimental import pallas as pl
from jax.experimental.pallas import tpu as pltpu

F32 = jnp.float32
BF16 = jnp.bfloat16

EPS = 1e-6
LANES = 128
NEG = -1e30
VMEM_LIMIT = 56 * 1024 * 1024

X_HEADS = 4
X_HEAD_DIM = 256
X_WIDTH = 1024
SELF_WIDTH = 3072
A_NOPE = 128
A_ROPE = 64
A_VDIM = 128
A_HEADS = 24
A_Q_RANK = 1024
A_KV_RANK = 512
A_QPAD = 256
ROPE_THETA = 10000.0
B_HEAD_DIM = 64
B_HEADS = 48
B_KV_HEADS = 8
B_GROUP = 6
B_PAIRS = B_KV_HEADS // 2
WINDOW = 128
BLOCK = 128
N_BUCKETS = 32
MAX_EXACT = 16
MAX_DIST = 128


def _params(sem):
    return pltpu.CompilerParams(dimension_semantics=sem, vmem_limit_bytes=VMEM_LIMIT)


def _rms(x, g):
    ms = jnp.mean(x * x, axis=-1, keepdims=True)
    return x * lax.rsqrt(ms + EPS) * g


def _dot_nt(a, b):
    return lax.dot_general(a, b, (((1,), (1,)), ((), ())), preferred_element_type=F32)


def _rmsnorm_kernel(x_ref, g_ref, o_ref):
    o_ref[...] = _rms(x_ref[...].astype(F32), g_ref[...]).astype(o_ref.dtype)


def rmsnorm(x, g, out_dtype, name, tm=256):
    m, d = x.shape
    tm = min(tm, m)
    return pl.pallas_call(
        _rmsnorm_kernel,
        grid=(m // tm,),
        in_specs=[pl.BlockSpec((tm, d), lambda i: (i, 0)),
                  pl.BlockSpec((1, d), lambda i: (0, 0))],
        out_specs=pl.BlockSpec((tm, d), lambda i: (i, 0)),
        out_shape=jax.ShapeDtypeStruct((m, d), out_dtype),
        compiler_params=_params(("parallel",)),
        name=name,
    )(x, g.reshape(1, d).astype(F32))


def _matmul_kernel(a_ref, b_ref, o_ref):
    o_ref[...] = jnp.dot(a_ref[...], b_ref[...], preferred_element_type=F32).astype(o_ref.dtype)


def matmul(a, b, out_dtype, name, tm=1024, tn=1024):
    m, k = a.shape
    n = b.shape[1]
    tm, tn = min(tm, m), min(tn, n)
    return pl.pallas_call(
        _matmul_kernel,
        grid=(m // tm, n // tn),
        in_specs=[pl.BlockSpec((tm, k), lambda i, j: (i, 0)),
                  pl.BlockSpec((k, tn), lambda i, j: (0, j))],
        out_specs=pl.BlockSpec((tm, tn), lambda i, j: (i, j)),
        out_shape=jax.ShapeDtypeStruct((m, n), out_dtype),
        compiler_params=_params(("parallel", "arbitrary")),
        name=name,
    )(a, b)


def _rope(u, ct, st):
    return u * ct + pltpu.roll(u, 64, 1) * st


def _qb_kernel(cq_ref, g_ref, w_ref, ct_ref, st_ref, o_ref, cqn_ref, *, heads, scale):
    @pl.when(pl.program_id(1) == 0)
    def _():
        cqn_ref[...] = _rms(cq_ref[...].astype(F32), g_ref[...]).astype(BF16)

    acc = jnp.dot(cqn_ref[...], w_ref[...], preferred_element_type=F32)
    ct = ct_ref[...] * scale
    st = st_ref[...] * scale
    for h in range(heads):
        lo = h * A_QPAD
        o_ref[:, lo:lo + A_NOPE] = (acc[:, lo:lo + A_NOPE] * scale).astype(BF16)
        o_ref[:, lo + A_NOPE:lo + A_QPAD] = _rope(acc[:, lo + A_NOPE:lo + A_QPAD], ct, st).astype(BF16)


def mla_q(proj, g, w_qb, ct, st, cq_blk, tm=1024, heads=4):
    t = proj.shape[0]
    tm = min(tm, t)
    n = A_HEADS * A_QPAD
    tn = heads * A_QPAD
    scale = (A_NOPE + A_ROPE) ** -0.5
    return pl.pallas_call(
        functools.partial(_qb_kernel, heads=heads, scale=scale),
        grid=(t // tm, n // tn),
        in_specs=[pl.BlockSpec((tm, A_Q_RANK), lambda i, j: (i, cq_blk)),
                  pl.BlockSpec((1, A_Q_RANK), lambda i, j: (0, 0)),
                  pl.BlockSpec((A_Q_RANK, tn), lambda i, j: (0, j)),
                  pl.BlockSpec((tm, LANES), lambda i, j: (i, 0)),
                  pl.BlockSpec((tm, LANES), lambda i, j: (i, 0))],
        out_specs=pl.BlockSpec((tm, tn), lambda i, j: (i, j)),
        out_shape=jax.ShapeDtypeStruct((t, n), BF16),
        scratch_shapes=[pltpu.VMEM((tm, A_Q_RANK), BF16)],
        compiler_params=_params(("parallel", "arbitrary")),
        name="mla_q_up",
    )(proj, g.reshape(1, -1).astype(F32), w_qb, ct, st)


def _kvb_kernel(ckv_ref, g_ref, w_ref, kr_ref, ct_ref, st_ref, kv_ref, kro_ref, cn_ref):
    @pl.when(pl.program_id(1) == 0)
    def _():
        cn_ref[...] = _rms(ckv_ref[...].astype(F32), g_ref[...]).astype(BF16)
        kro_ref[...] = _rope(kr_ref[...].astype(F32), ct_ref[...], st_ref[...]).astype(BF16)

    kv_ref[...] = jnp.dot(cn_ref[...], w_ref[...], preferred_element_type=F32).astype(BF16)


def mla_kv(proj, g, w_kvb, ct, st, ckv_blk, kr_blk, tm=1024, tn=1024):
    t = proj.shape[0]
    tm = min(tm, t)
    n = w_kvb.shape[1]
    return pl.pallas_call(
        _kvb_kernel,
        grid=(t // tm, n // tn),
        in_specs=[pl.BlockSpec((tm, A_KV_RANK), lambda i, j: (i, ckv_blk)),
                  pl.BlockSpec((1, A_KV_RANK), lambda i, j: (0, 0)),
                  pl.BlockSpec((A_KV_RANK, tn), lambda i, j: (0, j)),
                  pl.BlockSpec((tm, LANES), lambda i, j: (i, kr_blk)),
                  pl.BlockSpec((tm, LANES), lambda i, j: (i, 0)),
                  pl.BlockSpec((tm, LANES), lambda i, j: (i, 0))],
        out_specs=[pl.BlockSpec((tm, tn), lambda i, j: (i, j)),
                   pl.BlockSpec((tm, LANES), lambda i, j: (i, 0))],
        out_shape=[jax.ShapeDtypeStruct((t, n), BF16),
                   jax.ShapeDtypeStruct((t, LANES), BF16)],
        scratch_shapes=[pltpu.VMEM((tm, A_KV_RANK), BF16)],
        compiler_params=_params(("parallel", "arbitrary")),
        name="mla_kv_up",
    )(proj, g.reshape(1, -1).astype(F32), w_kvb, proj, ct, st)


def _mla_attn_kernel(q_ref, kn_ref, kr_ref, v_ref, o_ref, m_ref, l_ref, acc_ref, *, tq):
    qi = pl.program_id(2)
    q = q_ref[...]
    m_ref[...] = jnp.full(m_ref.shape, NEG, F32)
    l_ref[...] = jnp.zeros(l_ref.shape, F32)
    acc_ref[...] = jnp.zeros(acc_ref.shape, F32)

    def chunk(j, masked):
        rows = pl.ds(pl.multiple_of(j * tq, tq), tq)
        k = jnp.concatenate([kn_ref[rows, :], kr_ref[rows, :]], axis=1)
        s = _dot_nt(q, k)
        if masked:
            r = lax.broadcasted_iota(jnp.int32, s.shape, 0)
            c = lax.broadcasted_iota(jnp.int32, s.shape, 1)
            s = jnp.where(c <= r, s, NEG)
        m_prev = m_ref[...]
        m_cur = jnp.maximum(m_prev, jnp.max(s, axis=1, keepdims=True))
        alpha = jnp.exp(m_prev - m_cur)
        p = jnp.exp(s - m_cur)
        l_ref[...] = alpha * l_ref[...] + jnp.sum(p, axis=1, keepdims=True)
        acc_ref[...] = alpha * acc_ref[...] + jnp.dot(p.astype(BF16), v_ref[rows, :],
                                                      preferred_element_type=F32)
        m_ref[...] = m_cur

    def body(j, carry):
        chunk(j, False)
        return carry

    lax.fori_loop(0, qi, body, 0)
    chunk(qi, True)
    o_ref[...] = (acc_ref[...] / l_ref[...]).astype(o_ref.dtype)


def mla_attention(q, kv, kr, batch, seq, tq=512):
    t = q.shape[0]
    tq = min(tq, seq)
    nq = seq // tq
    return pl.pallas_call(
        functools.partial(_mla_attn_kernel, tq=tq),
        grid=(batch, A_HEADS, nq),
        in_specs=[pl.BlockSpec((tq, A_QPAD), lambda b, h, i: (b * nq + i, h)),
                  pl.BlockSpec((seq, A_NOPE), lambda b, h, i: (b, 2 * h)),
                  pl.BlockSpec((seq, LANES), lambda b, h, i: (b, 0)),
                  pl.BlockSpec((seq, A_VDIM), lambda b, h, i: (b, 2 * h + 1))],
        out_specs=pl.BlockSpec((tq, A_VDIM), lambda b, h, i: (b * nq + i, h)),
        out_shape=jax.ShapeDtypeStruct((t, A_HEADS * A_VDIM), BF16),
        scratch_shapes=[pltpu.VMEM((tq, 1), F32), pltpu.VMEM((tq, 1), F32),
                        pltpu.VMEM((tq, A_VDIM), F32)],
        compiler_params=_params(("parallel", "parallel", "arbitrary")),
        name="mla_attention",
    )(q, kv, kr, kv)


def _swa_kernel(q_ref, kc_ref, kp_ref, vc_ref, vp_ref, bias_ref, sink_ref, o_ref, kf_ref, vf_ref, *, tq):
    first = pl.program_id(2) == 0
    kf_ref[0:BLOCK, :] = kp_ref[...]
    kf_ref[BLOCK:, :] = kc_ref[...]
    vf_ref[0:BLOCK, :] = vp_ref[...]
    vf_ref[BLOCK:, :] = vc_ref[...]
    lane = lax.broadcasted_iota(jnp.int32, (BLOCK, LANES), 1)
    low = lane < B_HEAD_DIM
    col = lax.broadcasted_iota(jnp.int32, (1, 2 * BLOCK), 1)
    pen0 = jnp.where((col < BLOCK) & first, NEG, 0.0).astype(F32)
    for c in range(tq // BLOCK):
        band = pl.ds(c * BLOCK, 2 * BLOCK)
        kb = kf_ref[band, :]
        vb = vf_ref[band, :]
        for g in range(B_GROUP):
            qp = q_ref[c * BLOCK:(c + 1) * BLOCK, g * LANES:(g + 1) * LANES]
            zero = jnp.zeros_like(qp)
            q2 = jnp.concatenate([jnp.where(low, qp, zero), jnp.where(low, zero, qp)], axis=0)
            s = _dot_nt(q2, kb) + bias_ref[0, g]
            if c == 0:
                s = s + pen0
            sink = sink_ref[0, g]
            m = jnp.maximum(jnp.max(s, axis=1, keepdims=True), sink)
            p = jnp.exp(s - m)
            denom = jnp.sum(p, axis=1, keepdims=True) + jnp.exp(sink - m)
            o2 = jnp.dot(p.astype(BF16), vb, preferred_element_type=F32) / denom
            o_ref[c * BLOCK:(c + 1) * BLOCK, g * LANES:(g + 1) * LANES] = jnp.where(
                low, o2[0:BLOCK], o2[BLOCK:]).astype(o_ref.dtype)


def swa_attention(proj, bias, sink, batch, seq, q_off, k_off, v_off, tq=512):
    t = proj.shape[0]
    tq = min(tq, seq)
    nq = seq // tq
    r = tq // BLOCK
    qw = B_GROUP * LANES

    def prev_map(p, b, i, off):
        return (jnp.maximum(b * (seq // BLOCK) + i * r - 1, b * (seq // BLOCK)), off + p)

    return pl.pallas_call(
        functools.partial(_swa_kernel, tq=tq),
        grid=(B_PAIRS, batch, nq),
        in_specs=[pl.BlockSpec((tq, qw), lambda p, b, i: (b * nq + i, q_off // B_GROUP + p)),
                  pl.BlockSpec((tq, LANES), lambda p, b, i: (b * nq + i, k_off + p)),
                  pl.BlockSpec((BLOCK, LANES), functools.partial(prev_map, off=k_off)),
                  pl.BlockSpec((tq, LANES), lambda p, b, i: (b * nq + i, v_off + p)),
                  pl.BlockSpec((BLOCK, LANES), functools.partial(prev_map, off=v_off)),
                  pl.BlockSpec((1, B_GROUP, 2 * BLOCK, 2 * BLOCK), lambda p, b, i: (p, 0, 0, 0)),
                  pl.BlockSpec((1, B_GROUP, 2 * BLOCK, 1), lambda p, b, i: (p, 0, 0, 0))],
        out_specs=pl.BlockSpec((tq, qw), lambda p, b, i: (b * nq + i, p)),
        out_shape=jax.ShapeDtypeStruct((t, SELF_WIDTH), BF16),
        scratch_shapes=[pltpu.VMEM((tq + BLOCK, LANES), BF16), pltpu.VMEM((tq + BLOCK, LANES), BF16)],
        compiler_params=_params(("parallel", "parallel", "arbitrary")),
        name="swa_attention",
    )(proj, proj, proj, proj, proj, bias, sink)


def _memattn_kernel(xq_ref, mk_ref, mv_ref, o_ref):
    for h in range(X_HEADS):
        cols = slice(h * X_HEAD_DIM, (h + 1) * X_HEAD_DIM)
        s = _dot_nt(xq_ref[:, cols], mk_ref[:, cols])
        m = jnp.max(s, axis=1, keepdims=True)
        p = jnp.exp(s - m)
        l = jnp.sum(p, axis=1, keepdims=True)
        o = jnp.dot(p.astype(BF16), mv_ref[:, cols], preferred_element_type=F32) / l
        o_ref[:, cols] = o.astype(o_ref.dtype)


def mem_attention(proj, mkv, batch, seq, xq_blk, tq=512):
    t = proj.shape[0]
    n_mem = mkv.shape[0] // batch
    tq = min(tq, seq)
    nq = seq // tq
    return pl.pallas_call(
        _memattn_kernel,
        grid=(batch, nq),
        in_specs=[pl.BlockSpec((tq, X_WIDTH), lambda b, i: (b * nq + i, xq_blk)),
                  pl.BlockSpec((n_mem, X_WIDTH), lambda b, i: (b, 0)),
                  pl.BlockSpec((n_mem, X_WIDTH), lambda b, i: (b, 1))],
        out_specs=pl.BlockSpec((tq, X_WIDTH), lambda b, i: (b * nq + i, 0)),
        out_shape=jax.ShapeDtypeStruct((t, X_WIDTH), BF16),
        compiler_params=_params(("parallel", "arbitrary")),
        name="mem_attention",
    )(proj, mkv, mkv)


def _outproj_kernel(self_ref, mem_ref, z_ref, x_ref, w_ref, o_ref, y_ref):
    @pl.when(pl.program_id(1) == 0)
    def _():
        for c in range(0, y_ref.shape[1], X_WIDTH):
            z = z_ref[:, c:c + X_WIDTH].astype(F32)
            a = self_ref[:, c:c + X_WIDTH] if c < SELF_WIDTH else mem_ref[...]
            y_ref[:, c:c + X_WIDTH] = (a.astype(F32) * (z * jax.nn.sigmoid(z))).astype(BF16)

    o_ref[...] = x_ref[...] + jnp.dot(y_ref[...], w_ref[...], preferred_element_type=F32)


def out_proj(self_out, mem_out, proj, x, w_out, z_blk, tm=512, tn=1024):
    t, d = x.shape
    tm = min(tm, t)
    return pl.pallas_call(
        _outproj_kernel,
        grid=(t // tm, d // tn),
        in_specs=[pl.BlockSpec((tm, SELF_WIDTH), lambda i, j: (i, 0)),
                  pl.BlockSpec((tm, X_WIDTH), lambda i, j: (i, 0)),
                  pl.BlockSpec((tm, d), lambda i, j: (i, z_blk)),
                  pl.BlockSpec((tm, tn), lambda i, j: (i, j)),
                  pl.BlockSpec((d, tn), lambda i, j: (0, j))],
        out_specs=pl.BlockSpec((tm, tn), lambda i, j: (i, j)),
        out_shape=jax.ShapeDtypeStruct((t, d), F32),
        scratch_shapes=[pltpu.VMEM((tm, d), BF16)],
        compiler_params=_params(("parallel", "arbitrary")),
        name="gated_out_proj",
    )(self_out, mem_out, proj, x, w_out)


def _rope_tables(positions):
    inv = ROPE_THETA ** (-jnp.arange(0, A_ROPE, 2, dtype=F32) / A_ROPE)
    ang = positions.astype(F32).reshape(-1)[:, None] * inv
    cos, sin = jnp.cos(ang), jnp.sin(ang)
    zero = jnp.zeros_like(cos)
    ct = jnp.concatenate([cos, cos, zero, zero], axis=1)
    st = jnp.concatenate([-sin, sin, zero, zero], axis=1)
    return ct, st


def _swap_halves(w):
    half = w.shape[-1] // 2
    return jnp.concatenate([w[..., half:], w[..., :half]], axis=-1)


def _prep_a_w_in(w):
    o1, o2, o3, o4 = A_Q_RANK, A_Q_RANK + A_KV_RANK, A_Q_RANK + A_KV_RANK + A_ROPE, A_Q_RANK + A_KV_RANK + A_ROPE + X_WIDTH
    kr = w[:, o2:o3]
    pad = jnp.zeros((w.shape[0], 384), w.dtype)
    out = jnp.concatenate([w[:, o4:], w[:, :o1], w[:, o3:o4] * (X_HEAD_DIM ** -0.5), w[:, o1:o2],
                           kr, _swap_halves(kr), pad], axis=1)
    return out.astype(BF16)


def _prep_a_w_qb(w):
    w = w.reshape(A_Q_RANK, A_HEADS, A_NOPE + A_ROPE)
    rope = w[..., A_NOPE:]
    out = jnp.concatenate([w[..., :A_NOPE], rope, _swap_halves(rope)], axis=-1)
    return out.reshape(A_Q_RANK, A_HEADS * A_QPAD).astype(BF16)


def _b_head_order():
    return [(2 * p + e) * B_GROUP + g for p in range(B_PAIRS) for g in range(B_GROUP) for e in range(2)]


def _prep_b_w_in(w):
    nq = B_HEADS * B_HEAD_DIM
    nk = B_KV_HEADS * B_HEAD_DIM
    q = w[:, :nq].reshape(-1, B_HEADS, B_HEAD_DIM)[:, jnp.array(_b_head_order())].reshape(-1, nq)
    k = w[:, nq:nq + nk]
    v = w[:, nq + nk:nq + 2 * nk]
    xq = w[:, nq + 2 * nk:nq + 2 * nk + X_WIDTH]
    z = w[:, nq + 2 * nk + X_WIDTH:]
    z = jnp.concatenate([z[:, :nq].reshape(-1, B_HEADS, B_HEAD_DIM)[:, jnp.array(_b_head_order())].reshape(-1, nq),
                         z[:, nq:]], axis=1)
    out = jnp.concatenate([q * (B_HEAD_DIM ** -0.5), xq * (X_HEAD_DIM ** -0.5), z, k, v], axis=1)
    return out.astype(BF16)


def _prep_b_w_out(w):
    rows = w[:SELF_WIDTH].reshape(B_HEADS, B_HEAD_DIM, -1)[jnp.array(_b_head_order())].reshape(SELF_WIDTH, -1)
    return jnp.concatenate([rows, w[SELF_WIDTH:]], axis=0).astype(BF16)


def _t5_bucket(dist):
    n = jnp.maximum(dist, 0)
    nf = jnp.maximum(n, 1).astype(F32)
    large = MAX_EXACT + (jnp.log(nf / MAX_EXACT) / math.log(MAX_DIST / MAX_EXACT)
                         * (N_BUCKETS - MAX_EXACT)).astype(jnp.int32)
    large = jnp.minimum(large, N_BUCKETS - 1)
    return jnp.where(n < MAX_EXACT, n, large)


def _swa_tables(rel_bias, sinks):
    q_local = jnp.arange(BLOCK)[:, None]
    k_local = jnp.arange(2 * BLOCK)[None, :]
    dist = q_local + BLOCK - k_local
    in_window = (dist >= 0) & (dist < WINDOW)
    bias = rel_bias.astype(F32)[_t5_bucket(dist)]
    bias = jnp.where(in_window[:, :, None], bias, NEG).transpose(2, 0, 1)
    order = jnp.array(_b_head_order())
    bias = bias[order].reshape(B_PAIRS, B_GROUP, 2 * BLOCK, 2 * BLOCK)
    sink = jnp.broadcast_to(sinks.astype(F32)[order][:, None], (B_HEADS, BLOCK))
    sink = sink.reshape(B_PAIRS, B_GROUP, 2 * BLOCK, 1)
    return bias, sink


def kernel(x, mem, positions, norm_g, mem_norm_g, final_norm_g, w_mem_kv, w_out, a_w_in, a_q_norm_g, a_kv_norm_g,
           a_w_qb, a_w_kvb, b_w_in, b_sinks, rel_bias):
    batch, seq, d = x.shape
    n_mem = mem.shape[1]
    depth = norm_g.shape[0]
    xs = x.reshape(batch * seq, d)
    mems = mem.reshape(batch * n_mem, d)
    ct, st = _rope_tables(positions)

    for i in range(depth):
        j = i // 2
        h = rmsnorm(xs, norm_g[i], BF16, "x_rmsnorm")
        mn = rmsnorm(mems, mem_norm_g[i], BF16, "mem_rmsnorm")
        mkv = matmul(mn, w_mem_kv[i].astype(BF16), BF16, "mem_kv_proj")
        if i % 2 == 0:
            proj = matmul(h, _prep_a_w_in(a_w_in[j]), BF16, "mla_in_proj")
            q = mla_q(proj, a_q_norm_g[j], _prep_a_w_qb(a_w_qb[j]), ct, st, cq_blk=4)
            kv, kr = mla_kv(proj, a_kv_norm_g[j], a_w_kvb[j].astype(BF16), ct, st, ckv_blk=12, kr_blk=52)
            self_out = mla_attention(q, kv, kr, batch, seq)
            xq_blk, z_blk = 5, 0
            w_o = w_out[i].astype(BF16)
        else:
            proj = matmul(h, _prep_b_w_in(b_w_in[j]), BF16, "swa_in_proj")
            bias, sink = _swa_tables(rel_bias, b_sinks[j])
            self_out = swa_attention(proj, bias, sink, batch, seq, q_off=0, k_off=64, v_off=68)
            xq_blk, z_blk = 3, 1
            w_o = _prep_b_w_out(w_out[i])
        mem_out = mem_attention(proj, mkv, batch, seq, xq_blk)
        xs = out_proj(self_out, mem_out, proj, xs, w_o, z_blk)
    out = rmsnorm(xs, final_norm_g, x.dtype, "final_rmsnorm")
    return out.reshape(batch, seq, d)
```

```python
import functools
import math

import jax
import jax.numpy as jnp
from jax import lax
from jax.experimental import pallas as pl
from jax.experimental.pallas import tpu as pltpu

F32 = jnp.float32
BF16 = jnp.bfloat16

EPS = 1e-6
LANES = 128
NEG = -1e30
VMEM_LIMIT = 56 * 1024 * 1024

X_HEADS = 4
X_HEAD_DIM = 256
X_WIDTH = 1024
SELF_WIDTH = 3072
A_NOPE = 128
A_ROPE = 64
A_VDIM = 128
A_HEADS = 24
A_Q_RANK = 1024
A_KV_RANK = 512
SUM_ROWS = 16
A_QPAD = 256
ROPE_THETA = 10000.0
B_HEAD_DIM = 64
B_HEADS = 48
B_KV_HEADS = 8
B_GROUP = 6
B_PAIRS = B_KV_HEADS // 2
WINDOW = 128
BLOCK = 128
N_BUCKETS = 32
MAX_EXACT = 16
MAX_DIST = 128


def _params(sem):
    return pltpu.CompilerParams(dimension_semantics=sem, vmem_limit_bytes=VMEM_LIMIT)


def _rms(x, g):
    ms = jnp.mean(x * x, axis=-1, keepdims=True)
    return x * lax.rsqrt(ms + EPS) * g


def _dot_nt(a, b):
    return lax.dot_general(a, b, (((1,), (1,)), ((), ())), preferred_element_type=F32)


def _rmsnorm_kernel(x_ref, g_ref, o_ref):
    o_ref[...] = _rms(x_ref[...].astype(F32), g_ref[...]).astype(o_ref.dtype)


def rmsnorm(x, g, out_dtype, name, tm=256):
    m, d = x.shape
    tm = min(tm, m)
    return pl.pallas_call(
        _rmsnorm_kernel,
        grid=(m // tm,),
        in_specs=[pl.BlockSpec((tm, d), lambda i: (i, 0)),
                  pl.BlockSpec((1, d), lambda i: (0, 0))],
        out_specs=pl.BlockSpec((tm, d), lambda i: (i, 0)),
        out_shape=jax.ShapeDtypeStruct((m, d), out_dtype),
        compiler_params=_params(("parallel",)),
        name=name,
    )(x, g.reshape(1, d).astype(F32))


def _matmul_kernel(a_ref, b_ref, o_ref):
    o_ref[...] = jnp.dot(a_ref[...], b_ref[...], preferred_element_type=F32).astype(o_ref.dtype)


def matmul(a, b, out_dtype, name, tm=1024, tn=1024):
    m, k = a.shape
    n = b.shape[1]
    tm, tn = min(tm, m), min(tn, n)
    return pl.pallas_call(
        _matmul_kernel,
        grid=(m // tm, n // tn),
        in_specs=[pl.BlockSpec((tm, k), lambda i, j: (i, 0)),
                  pl.BlockSpec((k, tn), lambda i, j: (0, j))],
        out_specs=pl.BlockSpec((tm, tn), lambda i, j: (i, j)),
        out_shape=jax.ShapeDtypeStruct((m, n), out_dtype),
        compiler_params=_params(("parallel", "arbitrary")),
        name=name,
    )(a, b)


def _rope(u, ct, st):
    return u * ct + pltpu.roll(u, 64, 1) * st


def _qb_kernel(cq_ref, g_ref, w_ref, ct_ref, st_ref, o_ref, cqn_ref, *, heads, scale):
    @pl.when(pl.program_id(1) == 0)
    def _():
        cqn_ref[...] = _rms(cq_ref[...].astype(F32), g_ref[...]).astype(BF16)

    acc = jnp.dot(cqn_ref[...], w_ref[...], preferred_element_type=F32)
    ct = ct_ref[...] * scale
    st = st_ref[...] * scale
    for h in range(heads):
        lo = h * A_QPAD
        o_ref[:, lo:lo + A_NOPE] = (acc[:, lo:lo + A_NOPE] * scale).astype(BF16)
        o_ref[:, lo + A_NOPE:lo + A_QPAD] = _rope(acc[:, lo + A_NOPE:lo + A_QPAD], ct, st).astype(BF16)


def mla_q(proj, g, w_qb, ct, st, cq_blk, tm=1024, heads=4):
    t = proj.shape[0]
    tm = min(tm, t)
    n = A_HEADS * A_QPAD
    tn = heads * A_QPAD
    scale = (A_NOPE + A_ROPE) ** -0.5 * math.log2(math.e)
    return pl.pallas_call(
        functools.partial(_qb_kernel, heads=heads, scale=scale),
        grid=(t // tm, n // tn),
        in_specs=[pl.BlockSpec((tm, A_Q_RANK), lambda i, j: (i, cq_blk)),
                  pl.BlockSpec((1, A_Q_RANK), lambda i, j: (0, 0)),
                  pl.BlockSpec((A_Q_RANK, tn), lambda i, j: (0, j)),
                  pl.BlockSpec((tm, LANES), lambda i, j: (i, 0)),
                  pl.BlockSpec((tm, LANES), lambda i, j: (i, 0))],
        out_specs=pl.BlockSpec((tm, tn), lambda i, j: (i, j)),
        out_shape=jax.ShapeDtypeStruct((t, n), BF16),
        scratch_shapes=[pltpu.VMEM((tm, A_Q_RANK), BF16)],
        compiler_params=_params(("parallel", "arbitrary")),
        name="mla_q_up",
    )(proj, g.reshape(1, -1).astype(F32), w_qb, ct, st)


def _kvb_kernel(ckv_ref, g_ref, wk_ref, wvt_ref, kr_ref, ct_ref, st_ref, kn_ref, vt_ref, kro_ref, cn_ref):
    @pl.when(pl.program_id(1) == 0)
    def _():
        cn_ref[...] = _rms(ckv_ref[...].astype(F32), g_ref[...]).astype(BF16)
        kro_ref[...] = _rope(kr_ref[...].astype(F32), ct_ref[...], st_ref[...]).astype(BF16)

    cn = cn_ref[...]
    kn_ref[...] = jnp.dot(cn, wk_ref[...], preferred_element_type=F32).astype(BF16)
    vt_ref[...] = _dot_nt(wvt_ref[...], cn).astype(BF16)


def mla_kv(proj, g, wk, wvt, ct, st, ckv_blk, kr_blk, tm=1024, tn=1024):
    t = proj.shape[0]
    tm = min(tm, t)
    n = wk.shape[1]
    return pl.pallas_call(
        _kvb_kernel,
        grid=(t // tm, n // tn),
        in_specs=[pl.BlockSpec((tm, A_KV_RANK), lambda i, j: (i, ckv_blk)),
                  pl.BlockSpec((1, A_KV_RANK), lambda i, j: (0, 0)),
                  pl.BlockSpec((A_KV_RANK, tn), lambda i, j: (0, j)),
                  pl.BlockSpec((tn, A_KV_RANK), lambda i, j: (j, 0)),
                  pl.BlockSpec((tm, LANES), lambda i, j: (i, kr_blk)),
                  pl.BlockSpec((tm, LANES), lambda i, j: (i, 0)),
                  pl.BlockSpec((tm, LANES), lambda i, j: (i, 0))],
        out_specs=[pl.BlockSpec((tm, tn), lambda i, j: (i, j)),
                   pl.BlockSpec((tn, tm), lambda i, j: (j, i)),
                   pl.BlockSpec((tm, LANES), lambda i, j: (i, 0))],
        out_shape=[jax.ShapeDtypeStruct((t, n), BF16),
                   jax.ShapeDtypeStruct((n, t), BF16),
                   jax.ShapeDtypeStruct((t, LANES), BF16)],
        scratch_shapes=[pltpu.VMEM((tm, A_KV_RANK), BF16)],
        compiler_params=_params(("parallel", "arbitrary")),
        name="mla_kv_up",
    )(proj, g.reshape(1, -1).astype(F32), wk, wvt, proj, ct, st)


def _mla_attn_kernel(q_ref, kn_ref, kr_ref, vt_ref, o_ref, qt_ref, sa_ref, sb_ref, m_ref, acc_ref, *, tq, heads):
    qi = pl.program_id(2)
    for h in range(heads):
        qt_ref[h] = q_ref[:, h * A_QPAD:(h + 1) * A_QPAD].astype(F32).T.astype(BF16)
    m_ref[...] = jnp.full(m_ref.shape, NEG, F32)
    acc_ref[...] = jnp.zeros(acc_ref.shape, F32)
    ones = jnp.ones((SUM_ROWS, tq), BF16)

    def scores(j, s_ref):
        rows = pl.ds(pl.multiple_of(j * tq, tq), tq)
        kr = kr_ref[rows, :]
        for h in range(heads):
            k = jnp.concatenate([kn_ref[rows, h * A_NOPE:(h + 1) * A_NOPE], kr], axis=1)
            s_ref[h] = jnp.dot(k, qt_ref[h], preferred_element_type=F32)

    def update(j, s_ref, masked):
        rows = pl.ds(pl.multiple_of(j * tq, tq), tq)
        for h in range(heads):
            s = s_ref[h]
            if masked:
                r = lax.broadcasted_iota(jnp.int32, s.shape, 0)
                c = lax.broadcasted_iota(jnp.int32, s.shape, 1)
                s = jnp.where(r <= c, s, NEG)
            m_prev = m_ref[h]
            m_cur = jnp.maximum(m_prev, jnp.max(s, axis=0, keepdims=True))
            alpha = jnp.exp2(m_prev - m_cur)
            p = jnp.exp2(s - m_cur).astype(BF16)
            va = jnp.concatenate([vt_ref[h * A_VDIM:(h + 1) * A_VDIM, rows], ones], axis=0)
            acc_ref[h] = alpha * acc_ref[h] + jnp.dot(va, p, preferred_element_type=F32)
            m_ref[h] = m_cur

    even = (qi % 2) == 1

    @pl.when(even)
    def _():
        scores(0, sb_ref)
        scores(1, sa_ref)
        update(0, sb_ref, False)

    @pl.when(jnp.logical_not(even))
    def _():
        scores(0, sa_ref)

    first = even.astype(jnp.int32)

    def pair(t, carry):
        c = first + 2 * t
        scores(c + 1, sb_ref)
        update(c, sa_ref, False)
        scores(c + 2, sa_ref)
        update(c + 1, sb_ref, False)
        return carry

    lax.fori_loop(0, (qi - first) // 2, pair, 0)
    update(qi, sa_ref, True)
    for h in range(heads):
        acc = acc_ref[h]
        o = acc[:A_VDIM] * (1.0 / acc[A_VDIM:A_VDIM + 1])
        o_ref[:, h * A_VDIM:(h + 1) * A_VDIM] = o.T.astype(o_ref.dtype)


def mla_attention(q, kn, vt, kr, batch, seq, tq=512, heads=2):
    t = q.shape[0]
    tq = min(tq, seq)
    nq = seq // tq
    return pl.pallas_call(
        functools.partial(_mla_attn_kernel, tq=tq, heads=heads),
        grid=(batch, A_HEADS // heads, nq),
        in_specs=[pl.BlockSpec((tq, heads * A_QPAD), lambda b, h, i: (b * nq + i, h)),
                  pl.BlockSpec((seq, heads * A_NOPE), lambda b, h, i: (b, h)),
                  pl.BlockSpec((seq, LANES), lambda b, h, i: (b, 0)),
                  pl.BlockSpec((heads * A_VDIM, seq), lambda b, h, i: (h, b))],
        out_specs=pl.BlockSpec((tq, heads * A_VDIM), lambda b, h, i: (b * nq + i, h)),
        out_shape=jax.ShapeDtypeStruct((t, A_HEADS * A_VDIM), BF16),
        scratch_shapes=[pltpu.VMEM((heads, A_QPAD, tq), BF16),
                        pltpu.VMEM((heads, tq, tq), F32), pltpu.VMEM((heads, tq, tq), F32),
                        pltpu.VMEM((heads, 1, tq), F32),
                        pltpu.VMEM((heads, A_VDIM + SUM_ROWS, tq), F32)],
        compiler_params=_params(("parallel", "parallel", "arbitrary")),
        name="mla_attention",
    )(q, kn, kr, vt)


def _swa_kernel(q_ref, kc_ref, kp_ref, vc_ref, vp_ref, bias_ref, sink_ref, o_ref, kf_ref, vf_ref, *, tq):
    first = pl.program_id(2) == 0
    kf_ref[0:BLOCK, :] = kp_ref[...]
    kf_ref[BLOCK:, :] = kc_ref[...]
    vf_ref[0:BLOCK, :] = vp_ref[...]
    vf_ref[BLOCK:, :] = vc_ref[...]
    lane = lax.broadcasted_iota(jnp.int32, (BLOCK, LANES), 1)
    low = lane < B_HEAD_DIM
    col = lax.broadcasted_iota(jnp.int32, (1, 2 * BLOCK), 1)
    pen0 = jnp.where((col < BLOCK) & first, NEG, 0.0).astype(F32)
    for c in range(tq // BLOCK):
        band = pl.ds(c * BLOCK, 2 * BLOCK)
        kb = kf_ref[band, :]
        vb = vf_ref[band, :]
        for g in range(B_GROUP):
            qp = q_ref[c * BLOCK:(c + 1) * BLOCK, g * LANES:(g + 1) * LANES]
            zero = jnp.zeros_like(qp)
            q2 = jnp.concatenate([jnp.where(low, qp, zero), jnp.where(low, zero, qp)], axis=0)
            s = _dot_nt(q2, kb) + bias_ref[0, g]
            if c == 0:
                s = s + pen0
            sink = sink_ref[0, g]
            m = jnp.maximum(jnp.max(s, axis=1, keepdims=True), sink)
            p = jnp.exp(s - m)
            denom = jnp.sum(p, axis=1, keepdims=True) + jnp.exp(sink - m)
            o2 = jnp.dot(p.astype(BF16), vb, preferred_element_type=F32) / denom
            o_ref[c * BLOCK:(c + 1) * BLOCK, g * LANES:(g + 1) * LANES] = jnp.where(
                low, o2[0:BLOCK], o2[BLOCK:]).astype(o_ref.dtype)


def swa_attention(proj, bias, sink, batch, seq, q_off, k_off, v_off, tq=512):
    t = proj.shape[0]
    tq = min(tq, seq)
    nq = seq // tq
    r = tq // BLOCK
    qw = B_GROUP * LANES

    def prev_map(p, b, i, off):
        return (jnp.maximum(b * (seq // BLOCK) + i * r - 1, b * (seq // BLOCK)), off + p)

    return pl.pallas_call(
        functools.partial(_swa_kernel, tq=tq),
        grid=(B_PAIRS, batch, nq),
        in_specs=[pl.BlockSpec((tq, qw), lambda p, b, i: (b * nq + i, q_off // B_GROUP + p)),
                  pl.BlockSpec((tq, LANES), lambda p, b, i: (b * nq + i, k_off + p)),
                  pl.BlockSpec((BLOCK, LANES), functools.partial(prev_map, off=k_off)),
                  pl.BlockSpec((tq, LANES), lambda p, b, i: (b * nq + i, v_off + p)),
                  pl.BlockSpec((BLOCK, LANES), functools.partial(prev_map, off=v_off)),
                  pl.BlockSpec((1, B_GROUP, 2 * BLOCK, 2 * BLOCK), lambda p, b, i: (p, 0, 0, 0)),
                  pl.BlockSpec((1, B_GROUP, 2 * BLOCK, 1), lambda p, b, i: (p, 0, 0, 0))],
        out_specs=pl.BlockSpec((tq, qw), lambda p, b, i: (b * nq + i, p)),
        out_shape=jax.ShapeDtypeStruct((t, SELF_WIDTH), BF16),
        scratch_shapes=[pltpu.VMEM((tq + BLOCK, LANES), BF16), pltpu.VMEM((tq + BLOCK, LANES), BF16)],
        compiler_params=_params(("parallel", "parallel", "arbitrary")),
        name="swa_attention",
    )(proj, proj, proj, proj, proj, bias, sink)


def _memattn_kernel(xq_ref, mk_ref, mv_ref, o_ref):
    for h in range(X_HEADS):
        cols = slice(h * X_HEAD_DIM, (h + 1) * X_HEAD_DIM)
        s = _dot_nt(xq_ref[:, cols], mk_ref[:, cols])
        m = jnp.max(s, axis=1, keepdims=True)
        p = jnp.exp(s - m)
        l = jnp.sum(p, axis=1, keepdims=True)
        o = jnp.dot(p.astype(BF16), mv_ref[:, cols], preferred_element_type=F32) / l
        o_ref[:, cols] = o.astype(o_ref.dtype)


def mem_attention(proj, mkv, batch, seq, xq_blk, tq=512):
    t = proj.shape[0]
    n_mem = mkv.shape[0] // batch
    tq = min(tq, seq)
    nq = seq // tq
    return pl.pallas_call(
        _memattn_kernel,
        grid=(batch, nq),
        in_specs=[pl.BlockSpec((tq, X_WIDTH), lambda b, i: (b * nq + i, xq_blk)),
                  pl.BlockSpec((n_mem, X_WIDTH), lambda b, i: (b, 0)),
                  pl.BlockSpec((n_mem, X_WIDTH), lambda b, i: (b, 1))],
        out_specs=pl.BlockSpec((tq, X_WIDTH), lambda b, i: (b * nq + i, 0)),
        out_shape=jax.ShapeDtypeStruct((t, X_WIDTH), BF16),
        compiler_params=_params(("parallel", "arbitrary")),
        name="mem_attention",
    )(proj, mkv, mkv)


def _outproj_kernel(self_ref, mem_ref, z_ref, x_ref, w_ref, o_ref, y_ref):
    @pl.when(pl.program_id(1) == 0)
    def _():
        for c in range(0, y_ref.shape[1], X_WIDTH):
            z = z_ref[:, c:c + X_WIDTH].astype(F32)
            a = self_ref[:, c:c + X_WIDTH] if c < SELF_WIDTH else mem_ref[...]
            y_ref[:, c:c + X_WIDTH] = (a.astype(F32) * (z * jax.nn.sigmoid(z))).astype(BF16)

    o_ref[...] = x_ref[...] + jnp.dot(y_ref[...], w_ref[...], preferred_element_type=F32)


def out_proj(self_out, mem_out, proj, x, w_out, z_blk, tm=512, tn=1024):
    t, d = x.shape
    tm = min(tm, t)
    return pl.pallas_call(
        _outproj_kernel,
        grid=(t // tm, d // tn),
        in_specs=[pl.BlockSpec((tm, SELF_WIDTH), lambda i, j: (i, 0)),
                  pl.BlockSpec((tm, X_WIDTH), lambda i, j: (i, 0)),
                  pl.BlockSpec((tm, d), lambda i, j: (i, z_blk)),
                  pl.BlockSpec((tm, tn), lambda i, j: (i, j)),
                  pl.BlockSpec((d, tn), lambda i, j: (0, j))],
        out_specs=pl.BlockSpec((tm, tn), lambda i, j: (i, j)),
        out_shape=jax.ShapeDtypeStruct((t, d), F32),
        scratch_shapes=[pltpu.VMEM((tm, d), BF16)],
        compiler_params=_params(("parallel", "arbitrary")),
        name="gated_out_proj",
    )(self_out, mem_out, proj, x, w_out)


def _rope_tables(positions):
    inv = ROPE_THETA ** (-jnp.arange(0, A_ROPE, 2, dtype=F32) / A_ROPE)
    ang = positions.astype(F32).reshape(-1)[:, None] * inv
    cos, sin = jnp.cos(ang), jnp.sin(ang)
    zero = jnp.zeros_like(cos)
    ct = jnp.concatenate([cos, cos, zero, zero], axis=1)
    st = jnp.concatenate([-sin, sin, zero, zero], axis=1)
    return ct, st


def _swap_halves(w):
    half = w.shape[-1] // 2
    return jnp.concatenate([w[..., half:], w[..., :half]], axis=-1)


def _prep_a_w_in(w):
    o1, o2, o3, o4 = A_Q_RANK, A_Q_RANK + A_KV_RANK, A_Q_RANK + A_KV_RANK + A_ROPE, A_Q_RANK + A_KV_RANK + A_ROPE + X_WIDTH
    kr = w[:, o2:o3]
    pad = jnp.zeros((w.shape[0], 384), w.dtype)
    out = jnp.concatenate([w[:, o4:], w[:, :o1], w[:, o3:o4] * (X_HEAD_DIM ** -0.5), w[:, o1:o2],
                           kr, _swap_halves(kr), pad], axis=1)
    return out.astype(BF16)


def _prep_a_w_qb(w):
    w = w.reshape(A_Q_RANK, A_HEADS, A_NOPE + A_ROPE)
    rope = w[..., A_NOPE:]
    out = jnp.concatenate([w[..., :A_NOPE], rope, _swap_halves(rope)], axis=-1)
    return out.reshape(A_Q_RANK, A_HEADS * A_QPAD).astype(BF16)


def _prep_a_w_kvb(w):
    w = w.reshape(A_KV_RANK, A_HEADS, A_NOPE + A_VDIM)
    wk = w[..., :A_NOPE].reshape(A_KV_RANK, A_HEADS * A_NOPE)
    wvt = w[..., A_NOPE:].reshape(A_KV_RANK, A_HEADS * A_VDIM).T
    return wk.astype(BF16), wvt.astype(BF16)


def _b_head_order():
    return [(2 * p + e) * B_GROUP + g for p in range(B_PAIRS) for g in range(B_GROUP) for e in range(2)]


def _prep_b_w_in(w):
    nq = B_HEADS * B_HEAD_DIM
    nk = B_KV_HEADS * B_HEAD_DIM
    q = w[:, :nq].reshape(-1, B_HEADS, B_HEAD_DIM)[:, jnp.array(_b_head_order())].reshape(-1, nq)
    k = w[:, nq:nq + nk]
    v = w[:, nq + nk:nq + 2 * nk]
    xq = w[:, nq + 2 * nk:nq + 2 * nk + X_WIDTH]
    z = w[:, nq + 2 * nk + X_WIDTH:]
    z = jnp.concatenate([z[:, :nq].reshape(-1, B_HEADS, B_HEAD_DIM)[:, jnp.array(_b_head_order())].reshape(-1, nq),
                         z[:, nq:]], axis=1)
    out = jnp.concatenate([q * (B_HEAD_DIM ** -0.5), xq * (X_HEAD_DIM ** -0.5), z, k, v], axis=1)
    return out.astype(BF16)


def _prep_b_w_out(w):
    rows = w[:SELF_WIDTH].reshape(B_HEADS, B_HEAD_DIM, -1)[jnp.array(_b_head_order())].reshape(SELF_WIDTH, -1)
    return jnp.concatenate([rows, w[SELF_WIDTH:]], axis=0).astype(BF16)


def _t5_bucket(dist):
    n = jnp.maximum(dist, 0)
    nf = jnp.maximum(n, 1).astype(F32)
    large = MAX_EXACT + (jnp.log(nf / MAX_EXACT) / math.log(MAX_DIST / MAX_EXACT)
                         * (N_BUCKETS - MAX_EXACT)).astype(jnp.int32)
    large = jnp.minimum(large, N_BUCKETS - 1)
    return jnp.where(n < MAX_EXACT, n, large)


def _swa_tables(rel_bias, sinks):
    q_local = jnp.arange(BLOCK)[:, None]
    k_local = jnp.arange(2 * BLOCK)[None, :]
    dist = q_local + BLOCK - k_local
    in_window = (dist >= 0) & (dist < WINDOW)
    bias = rel_bias.astype(F32)[_t5_bucket(dist)]
    bias = jnp.where(in_window[:, :, None], bias, NEG).transpose(2, 0, 1)
    order = jnp.array(_b_head_order())
    bias = bias[order].reshape(B_PAIRS, B_GROUP, 2 * BLOCK, 2 * BLOCK)
    sink = jnp.broadcast_to(sinks.astype(F32)[order][:, None], (B_HEADS, BLOCK))
    sink = sink.reshape(B_PAIRS, B_GROUP, 2 * BLOCK, 1)
    return bias, sink


def kernel(x, mem, positions, norm_g, mem_norm_g, final_norm_g, w_mem_kv, w_out, a_w_in, a_q_norm_g, a_kv_norm_g,
           a_w_qb, a_w_kvb, b_w_in, b_sinks, rel_bias):
    batch, seq, d = x.shape
    n_mem = mem.shape[1]
    depth = norm_g.shape[0]
    xs = x.reshape(batch * seq, d)
    mems = mem.reshape(batch * n_mem, d)
    ct, st = _rope_tables(positions)

    for i in range(depth):
        j = i // 2
        h = rmsnorm(xs, norm_g[i], BF16, "x_rmsnorm")
        mn = rmsnorm(mems, mem_norm_g[i], BF16, "mem_rmsnorm")
        mkv = matmul(mn, w_mem_kv[i].astype(BF16), BF16, "mem_kv_proj")
        if i % 2 == 0:
            proj = matmul(h, _prep_a_w_in(a_w_in[j]), BF16, "mla_in_proj")
            q = mla_q(proj, a_q_norm_g[j], _prep_a_w_qb(a_w_qb[j]), ct, st, cq_blk=4)
            wk, wvt = _prep_a_w_kvb(a_w_kvb[j])
            kn, vt, kr = mla_kv(proj, a_kv_norm_g[j], wk, wvt, ct, st, ckv_blk=12, kr_blk=52)
            self_out = mla_attention(q, kn, vt, kr, batch, seq)
            xq_blk, z_blk = 5, 0
            w_o = w_out[i].astype(BF16)
        else:
            proj = matmul(h, _prep_b_w_in(b_w_in[j]), BF16, "swa_in_proj")
            bias, sink = _swa_tables(rel_bias, b_sinks[j])
            self_out = swa_attention(proj, bias, sink, batch, seq, q_off=0, k_off=64, v_off=68)
            xq_blk, z_blk = 3, 1
            w_o = _prep_b_w_out(w_out[i])
        mem_out = mem_attention(proj, mkv, batch, seq, xq_blk)
        xs = out_proj(self_out, mem_out, proj, xs, w_o, z_blk)
    out = rmsnorm(xs, final_norm_g, x.dtype, "final_rmsnorm")
    return out.reshape(batch, seq, d)
```

```python
import functools
import math

import jax
import jax.numpy as jnp
from jax import lax
from jax.experimental import pallas as pl
from jax.experimental.pallas import tpu as pltpu

F32 = jnp.float32
BF16 = jnp.bfloat16

EPS = 1e-6
LANES = 128
NEG = -1e30
VMEM_LIMIT = 56 * 1024 * 1024

X_HEADS = 4
X_HEAD_DIM = 256
X_WIDTH = 1024
SELF_WIDTH = 3072
A_NOPE = 128
A_ROPE = 64
A_VDIM = 128
A_HEADS = 24
A_Q_RANK = 1024
A_KV_RANK = 512
SUM_ROWS = 16
A_QPAD = 256
ROPE_THETA = 10000.0
B_HEAD_DIM = 64
B_HEADS = 48
B_KV_HEADS = 8
B_GROUP = 6
B_PAIRS = B_KV_HEADS // 2
WINDOW = 128
BLOCK = 128
N_BUCKETS = 32
MAX_EXACT = 16
MAX_DIST = 128


def _params(sem):
    return pltpu.CompilerParams(dimension_semantics=sem, vmem_limit_bytes=VMEM_LIMIT)


def _rms(x, g):
    ms = jnp.mean(x * x, axis=-1, keepdims=True)
    return x * lax.rsqrt(ms + EPS) * g


def _silu(z):
    return z * jax.nn.sigmoid(z)


def _dot_nt(a, b):
    return lax.dot_general(a, b, (((1,), (1,)), ((), ())), preferred_element_type=F32)


def _rmsnorm_kernel(x_ref, g_ref, o_ref):
    o_ref[...] = _rms(x_ref[...].astype(F32), g_ref[...]).astype(o_ref.dtype)


def rmsnorm(x, g, out_dtype, name, tm=256):
    m, d = x.shape
    tm = min(tm, m)
    return pl.pallas_call(
        _rmsnorm_kernel,
        grid=(m // tm,),
        in_specs=[pl.BlockSpec((tm, d), lambda i: (i, 0)),
                  pl.BlockSpec((1, d), lambda i: (0, 0))],
        out_specs=pl.BlockSpec((tm, d), lambda i: (i, 0)),
        out_shape=jax.ShapeDtypeStruct((m, d), out_dtype),
        compiler_params=_params(("parallel",)),
        name=name,
    )(x, g.reshape(1, d).astype(F32))


def _matmul_kernel(a_ref, b_ref, o_ref):
    o_ref[...] = jnp.dot(a_ref[...], b_ref[...], preferred_element_type=F32).astype(o_ref.dtype)


def matmul(a, b, out_dtype, name, tm=1024, tn=1024):
    m, k = a.shape
    n = b.shape[1]
    tm, tn = min(tm, m), min(tn, n)
    return pl.pallas_call(
        _matmul_kernel,
        grid=(m // tm, n // tn),
        in_specs=[pl.BlockSpec((tm, k), lambda i, j: (i, 0)),
                  pl.BlockSpec((k, tn), lambda i, j: (0, j))],
        out_specs=pl.BlockSpec((tm, tn), lambda i, j: (i, j)),
        out_shape=jax.ShapeDtypeStruct((m, n), out_dtype),
        compiler_params=_params(("parallel", "arbitrary")),
        name=name,
    )(a, b)


def _rope(u, ct, st):
    return u * ct + pltpu.roll(u, 64, 1) * st


def _qb_kernel(cq_ref, g_ref, w_ref, ct_ref, st_ref, o_ref, cqn_ref, *, heads, scale):
    @pl.when(pl.program_id(1) == 0)
    def _():
        cqn_ref[...] = _rms(cq_ref[...].astype(F32), g_ref[...]).astype(BF16)

    acc = jnp.dot(cqn_ref[...], w_ref[...], preferred_element_type=F32)
    ct = ct_ref[...] * scale
    st = st_ref[...] * scale
    for h in range(heads):
        lo = h * A_QPAD
        o_ref[:, lo:lo + A_NOPE] = (acc[:, lo:lo + A_NOPE] * scale).astype(BF16)
        o_ref[:, lo + A_NOPE:lo + A_QPAD] = _rope(acc[:, lo + A_NOPE:lo + A_QPAD], ct, st).astype(BF16)


def mla_q(proj, g, w_qb, ct, st, cq_blk, tm=1024, heads=4):
    t = proj.shape[0]
    tm = min(tm, t)
    n = A_HEADS * A_QPAD
    tn = heads * A_QPAD
    scale = (A_NOPE + A_ROPE) ** -0.5 * math.log2(math.e)
    return pl.pallas_call(
        functools.partial(_qb_kernel, heads=heads, scale=scale),
        grid=(t // tm, n // tn),
        in_specs=[pl.BlockSpec((tm, A_Q_RANK), lambda i, j: (i, cq_blk)),
                  pl.BlockSpec((1, A_Q_RANK), lambda i, j: (0, 0)),
                  pl.BlockSpec((A_Q_RANK, tn), lambda i, j: (0, j)),
                  pl.BlockSpec((tm, LANES), lambda i, j: (i, 0)),
                  pl.BlockSpec((tm, LANES), lambda i, j: (i, 0))],
        out_specs=pl.BlockSpec((tm, tn), lambda i, j: (i, j)),
        out_shape=jax.ShapeDtypeStruct((t, n), BF16),
        scratch_shapes=[pltpu.VMEM((tm, A_Q_RANK), BF16)],
        compiler_params=_params(("parallel", "arbitrary")),
        name="mla_q_up",
    )(proj, g.reshape(1, -1).astype(F32), w_qb, ct, st)


def _kvb_kernel(ckv_ref, g_ref, wk_ref, wvt_ref, kr_ref, ct_ref, st_ref, kn_ref, vt_ref, kro_ref, cn_ref):
    @pl.when(pl.program_id(1) == 0)
    def _():
        cn_ref[...] = _rms(ckv_ref[...].astype(F32), g_ref[...]).astype(BF16)
        kro_ref[...] = _rope(kr_ref[...].astype(F32), ct_ref[...], st_ref[...]).astype(BF16)

    cn = cn_ref[...]
    kn_ref[...] = jnp.dot(cn, wk_ref[...], preferred_element_type=F32).astype(BF16)
    vt_ref[...] = _dot_nt(wvt_ref[...], cn).astype(BF16)


def mla_kv(proj, g, wk, wvt, ct, st, ckv_blk, kr_blk, tm=1024, tn=1024):
    t = proj.shape[0]
    tm = min(tm, t)
    n = wk.shape[1]
    return pl.pallas_call(
        _kvb_kernel,
        grid=(t // tm, n // tn),
        in_specs=[pl.BlockSpec((tm, A_KV_RANK), lambda i, j: (i, ckv_blk)),
                  pl.BlockSpec((1, A_KV_RANK), lambda i, j: (0, 0)),
                  pl.BlockSpec((A_KV_RANK, tn), lambda i, j: (0, j)),
                  pl.BlockSpec((tn, A_KV_RANK), lambda i, j: (j, 0)),
                  pl.BlockSpec((tm, LANES), lambda i, j: (i, kr_blk)),
                  pl.BlockSpec((tm, LANES), lambda i, j: (i, 0)),
                  pl.BlockSpec((tm, LANES), lambda i, j: (i, 0))],
        out_specs=[pl.BlockSpec((tm, tn), lambda i, j: (i, j)),
                   pl.BlockSpec((tn, tm), lambda i, j: (j, i)),
                   pl.BlockSpec((tm, LANES), lambda i, j: (i, 0))],
        out_shape=[jax.ShapeDtypeStruct((t, n), BF16),
                   jax.ShapeDtypeStruct((n, t), BF16),
                   jax.ShapeDtypeStruct((t, LANES), BF16)],
        scratch_shapes=[pltpu.VMEM((tm, A_KV_RANK), BF16)],
        compiler_params=_params(("parallel", "arbitrary")),
        name="mla_kv_up",
    )(proj, g.reshape(1, -1).astype(F32), wk, wvt, proj, ct, st)


def _mla_attn_kernel(q_ref, z_ref, kn_ref, kr_ref, vt_ref, o_ref, qt_ref, sa_ref, sb_ref, m_ref, acc_ref, *,
                     tq, heads):
    qi = pl.program_id(2)
    for h in range(heads):
        qt_ref[h] = q_ref[:, h * A_QPAD:(h + 1) * A_QPAD].astype(F32).T.astype(BF16)
    m_ref[...] = jnp.full(m_ref.shape, NEG, F32)
    acc_ref[...] = jnp.zeros(acc_ref.shape, F32)
    ones = jnp.ones((SUM_ROWS, tq), BF16)

    def scores(j, s_ref):
        rows = pl.ds(pl.multiple_of(j * tq, tq), tq)
        kr = kr_ref[rows, :]
        for h in range(heads):
            k = jnp.concatenate([kn_ref[rows, h * A_NOPE:(h + 1) * A_NOPE], kr], axis=1)
            s_ref[h] = jnp.dot(k, qt_ref[h], preferred_element_type=F32)

    def update(j, s_ref, masked):
        rows = pl.ds(pl.multiple_of(j * tq, tq), tq)
        for h in range(heads):
            s = s_ref[h]
            if masked:
                r = lax.broadcasted_iota(jnp.int32, s.shape, 0)
                c = lax.broadcasted_iota(jnp.int32, s.shape, 1)
                s = jnp.where(r <= c, s, NEG)
            m_prev = m_ref[h]
            m_cur = jnp.maximum(m_prev, jnp.max(s, axis=0, keepdims=True))
            alpha = jnp.exp2(m_prev - m_cur)
            p = jnp.exp2(s - m_cur).astype(BF16)
            va = jnp.concatenate([vt_ref[h * A_VDIM:(h + 1) * A_VDIM, rows], ones], axis=0)
            acc_ref[h] = alpha * acc_ref[h] + jnp.dot(va, p, preferred_element_type=F32)
            m_ref[h] = m_cur

    even = (qi % 2) == 1

    @pl.when(even)
    def _():
        scores(0, sb_ref)
        scores(1, sa_ref)
        update(0, sb_ref, False)

    @pl.when(jnp.logical_not(even))
    def _():
        scores(0, sa_ref)

    first = even.astype(jnp.int32)

    def pair(t, carry):
        c = first + 2 * t
        scores(c + 1, sb_ref)
        update(c, sa_ref, False)
        scores(c + 2, sa_ref)
        update(c + 1, sb_ref, False)
        return carry

    lax.fori_loop(0, (qi - first) // 2, pair, 0)
    update(qi, sa_ref, True)
    for h in range(heads):
        acc = acc_ref[h]
        o = (acc[:A_VDIM] * (1.0 / acc[A_VDIM:A_VDIM + 1])).T
        gate = _silu(z_ref[:, h * A_VDIM:(h + 1) * A_VDIM].astype(F32))
        o_ref[:, h * A_VDIM:(h + 1) * A_VDIM] = (o * gate).astype(o_ref.dtype)


def mla_attention(q, proj, kn, vt, kr, batch, seq, tq=512, heads=4):
    t = q.shape[0]
    tq = min(tq, seq)
    nq = seq // tq
    return pl.pallas_call(
        functools.partial(_mla_attn_kernel, tq=tq, heads=heads),
        grid=(batch, A_HEADS // heads, nq),
        in_specs=[pl.BlockSpec((tq, heads * A_QPAD), lambda b, h, i: (b * nq + i, h)),
                  pl.BlockSpec((tq, heads * A_VDIM), lambda b, h, i: (b * nq + i, h)),
                  pl.BlockSpec((seq, heads * A_NOPE), lambda b, h, i: (b, h)),
                  pl.BlockSpec((seq, LANES), lambda b, h, i: (b, 0)),
                  pl.BlockSpec((heads * A_VDIM, seq), lambda b, h, i: (h, b))],
        out_specs=pl.BlockSpec((tq, heads * A_VDIM), lambda b, h, i: (b * nq + i, h)),
        out_shape=jax.ShapeDtypeStruct((t, A_HEADS * A_VDIM), BF16),
        scratch_shapes=[pltpu.VMEM((heads, A_QPAD, tq), BF16),
                        pltpu.VMEM((heads, tq, tq), F32), pltpu.VMEM((heads, tq, tq), F32),
                        pltpu.VMEM((heads, 1, tq), F32),
                        pltpu.VMEM((heads, A_VDIM + SUM_ROWS, tq), F32)],
        compiler_params=_params(("parallel", "parallel", "arbitrary")),
        name="mla_attention",
    )(q, proj, kn, kr, vt)


def _swa_kernel(q_ref, z_ref, kc_ref, kp_ref, vc_ref, vp_ref, bias_ref, sink_ref, o_ref,
                kf_ref, vf_ref, vt_ref, s_ref, p_ref, inv_ref, *, tq):
    first = pl.program_id(2) == 0
    kf_ref[0:BLOCK, :] = kp_ref[...]
    kf_ref[BLOCK:, :] = kc_ref[...]
    vf_ref[0:BLOCK, :] = vp_ref[...]
    vf_ref[BLOCK:, :] = vc_ref[...]
    low = lax.broadcasted_iota(jnp.int32, (BLOCK, LANES), 1) < B_HEAD_DIM
    key = lax.broadcasted_iota(jnp.int32, (2 * BLOCK, 2 * BLOCK), 0)
    pen0 = jnp.where((key < BLOCK) & first, NEG, 0.0).astype(F32)
    tiles = [(c, g) for c in range(tq // BLOCK) for g in range(B_GROUP)]
    for t, (c, g) in enumerate(tiles):
        qp = q_ref[c * BLOCK:(c + 1) * BLOCK, g * LANES:(g + 1) * LANES]
        zero = jnp.zeros_like(qp)
        q2 = jnp.concatenate([jnp.where(low, qp, zero), jnp.where(low, zero, qp)], axis=0)
        s = _dot_nt(kf_ref[c * BLOCK:(c + 2) * BLOCK, :], q2) + bias_ref[0, g]
        s_ref[t] = s + pen0 if c == 0 else s
    for t, (c, g) in enumerate(tiles):
        s = s_ref[t]
        sink = sink_ref[0, g]
        m = jnp.maximum(jnp.max(s, axis=0, keepdims=True), sink)
        p = jnp.exp(s - m)
        inv_ref[t] = 1.0 / (jnp.sum(p, axis=0, keepdims=True) + jnp.exp(sink - m))
        p_ref[t] = p.astype(BF16)
    for c in range(tq // BLOCK):
        vt_ref[c] = vf_ref[c * BLOCK:(c + 2) * BLOCK, :].astype(F32).T.astype(BF16)
    for t, (c, g) in enumerate(tiles):
        rows = slice(c * BLOCK, (c + 1) * BLOCK)
        lanes = slice(g * LANES, (g + 1) * LANES)
        ot = jnp.dot(vt_ref[c], p_ref[t], preferred_element_type=F32) * inv_ref[t]
        o = jnp.concatenate([ot[0:B_HEAD_DIM, 0:BLOCK], ot[B_HEAD_DIM:, BLOCK:]], axis=0).T
        o_ref[rows, lanes] = (o * _silu(z_ref[rows, lanes].astype(F32))).astype(o_ref.dtype)


def swa_attention(proj, bias, sink, batch, seq, tq=512):
    t = proj.shape[0]
    tq = min(tq, seq)
    nq = seq // tq
    r = tq // BLOCK
    qw = B_GROUP * LANES
    k_off, v_off = 8192 // LANES, 8704 // LANES

    def prev_map(p, b, i, off):
        return (jnp.maximum(b * (seq // BLOCK) + i * r - 1, b * (seq // BLOCK)), off + p)

    return pl.pallas_call(
        functools.partial(_swa_kernel, tq=tq),
        grid=(B_PAIRS, batch, nq),
        in_specs=[pl.BlockSpec((tq, qw), lambda p, b, i: (b * nq + i, p)),
                  pl.BlockSpec((tq, qw), lambda p, b, i: (b * nq + i, B_PAIRS + p)),
                  pl.BlockSpec((tq, LANES), lambda p, b, i: (b * nq + i, k_off + p)),
                  pl.BlockSpec((BLOCK, LANES), functools.partial(prev_map, off=k_off)),
                  pl.BlockSpec((tq, LANES), lambda p, b, i: (b * nq + i, v_off + p)),
                  pl.BlockSpec((BLOCK, LANES), functools.partial(prev_map, off=v_off)),
                  pl.BlockSpec((1, B_GROUP, 2 * BLOCK, 2 * BLOCK), lambda p, b, i: (p, 0, 0, 0)),
                  pl.BlockSpec((1, B_GROUP, 1, 2 * BLOCK), lambda p, b, i: (p, 0, 0, 0))],
        out_specs=pl.BlockSpec((tq, qw), lambda p, b, i: (b * nq + i, p)),
        out_shape=jax.ShapeDtypeStruct((t, SELF_WIDTH), BF16),
        scratch_shapes=[pltpu.VMEM((tq + BLOCK, LANES), BF16), pltpu.VMEM((tq + BLOCK, LANES), BF16),
                        pltpu.VMEM((r, LANES, 2 * BLOCK), BF16),
                        pltpu.VMEM((r * B_GROUP, 2 * BLOCK, 2 * BLOCK), F32),
                        pltpu.VMEM((r * B_GROUP, 2 * BLOCK, 2 * BLOCK), BF16),
                        pltpu.VMEM((r * B_GROUP, 1, 2 * BLOCK), F32)],
        compiler_params=_params(("parallel", "parallel", "arbitrary")),
        name="swa_attention",
    )(proj, proj, proj, proj, proj, proj, bias, sink)


def _memattn_kernel(xq_ref, z_ref, mk_ref, mv_ref, o_ref):
    for h in range(X_HEADS):
        cols = slice(h * X_HEAD_DIM, (h + 1) * X_HEAD_DIM)
        s = _dot_nt(xq_ref[:, cols], mk_ref[:, cols])
        m = jnp.max(s, axis=1, keepdims=True)
        p = jnp.exp(s - m)
        l = jnp.sum(p, axis=1, keepdims=True)
        o = jnp.dot(p.astype(BF16), mv_ref[:, cols], preferred_element_type=F32) / l
        o_ref[:, cols] = (o * _silu(z_ref[:, cols].astype(F32))).astype(o_ref.dtype)


def mem_attention(proj, mkv, batch, seq, xq_blk, z_blk, tq=512):
    t = proj.shape[0]
    n_mem = mkv.shape[0] // batch
    tq = min(tq, seq)
    nq = seq // tq
    return pl.pallas_call(
        _memattn_kernel,
        grid=(batch, nq),
        in_specs=[pl.BlockSpec((tq, X_WIDTH), lambda b, i: (b * nq + i, xq_blk)),
                  pl.BlockSpec((tq, X_WIDTH), lambda b, i: (b * nq + i, z_blk)),
                  pl.BlockSpec((n_mem, X_WIDTH), lambda b, i: (b, 0)),
                  pl.BlockSpec((n_mem, X_WIDTH), lambda b, i: (b, 1))],
        out_specs=pl.BlockSpec((tq, X_WIDTH), lambda b, i: (b * nq + i, 0)),
        out_shape=jax.ShapeDtypeStruct((t, X_WIDTH), BF16),
        compiler_params=_params(("parallel", "arbitrary")),
        name="mem_attention",
    )(proj, proj, mkv, mkv)


def _outproj_kernel(ys_ref, ym_ref, x_ref, ws_ref, wm_ref, o_ref):
    acc = jnp.dot(ys_ref[...], ws_ref[...], preferred_element_type=F32)
    acc = acc + jnp.dot(ym_ref[...], wm_ref[...], preferred_element_type=F32)
    o_ref[...] = x_ref[...] + acc


def out_proj(y_self, y_mem, x, w_out, tm=1024, tn=512):
    t, d = x.shape
    tm = min(tm, t)
    return pl.pallas_call(
        _outproj_kernel,
        grid=(t // tm, d // tn),
        in_specs=[pl.BlockSpec((tm, SELF_WIDTH), lambda i, j: (i, 0)),
                  pl.BlockSpec((tm, X_WIDTH), lambda i, j: (i, 0)),
                  pl.BlockSpec((tm, tn), lambda i, j: (i, j)),
                  pl.BlockSpec((SELF_WIDTH, tn), lambda i, j: (0, j)),
                  pl.BlockSpec((X_WIDTH, tn), lambda i, j: (SELF_WIDTH // X_WIDTH, j))],
        out_specs=pl.BlockSpec((tm, tn), lambda i, j: (i, j)),
        out_shape=jax.ShapeDtypeStruct((t, d), F32),
        compiler_params=_params(("parallel", "arbitrary")),
        name="out_proj",
    )(y_self, y_mem, x, w_out, w_out)


def _rope_tables(positions):
    inv = ROPE_THETA ** (-jnp.arange(0, A_ROPE, 2, dtype=F32) / A_ROPE)
    ang = positions.astype(F32).reshape(-1)[:, None] * inv
    cos, sin = jnp.cos(ang), jnp.sin(ang)
    zero = jnp.zeros_like(cos)
    ct = jnp.concatenate([cos, cos, zero, zero], axis=1)
    st = jnp.concatenate([-sin, sin, zero, zero], axis=1)
    return ct, st


def _swap_halves(w):
    half = w.shape[-1] // 2
    return jnp.concatenate([w[..., half:], w[..., :half]], axis=-1)


def _prep_a_w_in(w):
    o1, o2, o3, o4 = A_Q_RANK, A_Q_RANK + A_KV_RANK, A_Q_RANK + A_KV_RANK + A_ROPE, A_Q_RANK + A_KV_RANK + A_ROPE + X_WIDTH
    kr = w[:, o2:o3].astype(BF16)
    pad = jnp.zeros((w.shape[0], 384), BF16)
    return jnp.concatenate([w[:, o4:].astype(BF16), w[:, :o1].astype(BF16),
                            (w[:, o3:o4] * (X_HEAD_DIM ** -0.5)).astype(BF16), w[:, o1:o2].astype(BF16),
                            kr, _swap_halves(kr), pad], axis=1)


def _prep_a_w_qb(w):
    w = w.astype(BF16).reshape(A_Q_RANK, A_HEADS, A_NOPE + A_ROPE)
    rope = w[..., A_NOPE:]
    out = jnp.concatenate([w[..., :A_NOPE], rope, _swap_halves(rope)], axis=-1)
    return out.reshape(A_Q_RANK, A_HEADS * A_QPAD)


def _prep_a_w_kvb(w):
    w = w.astype(BF16).reshape(A_KV_RANK, A_HEADS, A_NOPE + A_VDIM)
    wk = w[..., :A_NOPE].reshape(A_KV_RANK, A_HEADS * A_NOPE)
    wvt = w[..., A_NOPE:].reshape(A_KV_RANK, A_HEADS * A_VDIM).T
    return wk, wvt


def _pair_major(w, axis):
    shape = w.shape
    w = w.reshape(shape[:axis] + (B_PAIRS, 2, B_GROUP, B_HEAD_DIM) + shape[axis + 1:])
    w = jnp.swapaxes(w, axis + 1, axis + 2)
    return w.reshape(shape)


def _prep_b_w_in(w):
    nq = B_HEADS * B_HEAD_DIM
    nk = B_KV_HEADS * B_HEAD_DIM
    zo = nq + 2 * nk + X_WIDTH
    q = _pair_major((w[:, :nq] * (B_HEAD_DIM ** -0.5)).astype(BF16), 1)
    zs = _pair_major(w[:, zo:zo + nq].astype(BF16), 1)
    xq = (w[:, nq + 2 * nk:zo] * (X_HEAD_DIM ** -0.5)).astype(BF16)
    return jnp.concatenate([q, zs, xq, w[:, zo + nq:].astype(BF16), w[:, nq:nq + 2 * nk].astype(BF16)], axis=1)


def _prep_b_w_out(w):
    w = w.astype(BF16)
    return jnp.concatenate([_pair_major(w[:SELF_WIDTH], 0), w[SELF_WIDTH:]], axis=0)


def _t5_bucket(dist):
    n = jnp.maximum(dist, 0)
    nf = jnp.maximum(n, 1).astype(F32)
    large = MAX_EXACT + (jnp.log(nf / MAX_EXACT) / math.log(MAX_DIST / MAX_EXACT)
                         * (N_BUCKETS - MAX_EXACT)).astype(jnp.int32)
    large = jnp.minimum(large, N_BUCKETS - 1)
    return jnp.where(n < MAX_EXACT, n, large)


def _swa_tables(rel_bias, sinks):
    q_local = jnp.arange(BLOCK)[None, :]
    k_local = jnp.arange(2 * BLOCK)[:, None]
    dist = q_local + BLOCK - k_local
    in_window = (dist >= 0) & (dist < WINDOW)
    bias = rel_bias.astype(F32)[_t5_bucket(dist)]
    bias = jnp.where(in_window[:, :, None], bias, NEG)
    bias = bias.reshape(2 * BLOCK, BLOCK, B_PAIRS, 2, B_GROUP)
    bias = bias.transpose(2, 4, 0, 3, 1).reshape(B_PAIRS, B_GROUP, 2 * BLOCK, 2 * BLOCK)
    sink = sinks.astype(F32).reshape(B_PAIRS, 2, B_GROUP).transpose(0, 2, 1)
    sink = jnp.broadcast_to(sink[..., None], (B_PAIRS, B_GROUP, 2, BLOCK)).reshape(B_PAIRS, B_GROUP, 1, 2 * BLOCK)
    return bias, sink


def kernel(x, mem, positions, norm_g, mem_norm_g, final_norm_g, w_mem_kv, w_out, a_w_in, a_q_norm_g, a_kv_norm_g,
           a_w_qb, a_w_kvb, b_w_in, b_sinks, rel_bias):
    batch, seq, d = x.shape
    n_mem = mem.shape[1]
    depth = norm_g.shape[0]
    xs = x.reshape(batch * seq, d)
    mems = mem.reshape(batch * n_mem, d)
    ct, st = _rope_tables(positions)

    for i in range(depth):
        j = i // 2
        h = rmsnorm(xs, norm_g[i], BF16, "x_rmsnorm")
        mn = rmsnorm(mems, mem_norm_g[i], BF16, "mem_rmsnorm")
        mkv = matmul(mn, w_mem_kv[i].astype(BF16), BF16, "mem_kv_proj")
        if i % 2 == 0:
            proj = matmul(h, _prep_a_w_in(a_w_in[j]), BF16, "mla_in_proj")
            q = mla_q(proj, a_q_norm_g[j], _prep_a_w_qb(a_w_qb[j]), ct, st, cq_blk=4)
            wk, wvt = _prep_a_w_kvb(a_w_kvb[j])
            kn, vt, kr = mla_kv(proj, a_kv_norm_g[j], wk, wvt, ct, st, ckv_blk=12, kr_blk=52)
            y_self = mla_attention(q, proj, kn, vt, kr, batch, seq)
            y_mem = mem_attention(proj, mkv, batch, seq, xq_blk=5, z_blk=3)
            w_o = w_out[i].astype(BF16)
        else:
            proj = matmul(h, _prep_b_w_in(b_w_in[j]), BF16, "swa_in_proj")
            bias, sink = _swa_tables(rel_bias, b_sinks[j])
            y_self = swa_attention(proj, bias, sink, batch, seq)
            y_mem = mem_attention(proj, mkv, batch, seq, xq_blk=6, z_blk=7)
            w_o = _prep_b_w_out(w_out[i])
        xs = out_proj(y_self, y_mem, xs, w_o)
    out = rmsnorm(xs, final_norm_g, x.dtype, "final_rmsnorm")
    return out.reshape(batch, seq, d)
```

```python
import functools
import math

import jax
import jax.numpy as jnp
from jax import lax
from jax.experimental import pallas as pl
from jax.experimental.pallas import tpu as pltpu

F32 = jnp.float32
BF16 = jnp.bfloat16

EPS = 1e-6
LANES = 128
NEG = -1e30
VMEM_LIMIT = 56 * 1024 * 1024

X_HEADS = 4
X_HEAD_DIM = 256
X_WIDTH = 1024
SELF_WIDTH = 3072
A_NOPE = 128
A_ROPE = 64
A_VDIM = 128
A_HEADS = 24
A_Q_RANK = 1024
A_KV_RANK = 512
SUM_ROWS = 16
A_QPAD = 256
ROPE_THETA = 10000.0
B_HEAD_DIM = 64
B_HEADS = 48
B_KV_HEADS = 8
B_GROUP = 6
B_PAIRS = B_KV_HEADS // 2
WINDOW = 128
BLOCK = 128
N_BUCKETS = 32
MAX_EXACT = 16
MAX_DIST = 128


def _params(sem):
    return pltpu.CompilerParams(dimension_semantics=sem, vmem_limit_bytes=VMEM_LIMIT)


def _rms(x, g):
    ms = jnp.mean(x * x, axis=-1, keepdims=True)
    return x * lax.rsqrt(ms + EPS) * g


def _silu(z):
    return z * jax.nn.sigmoid(z)


def _dot_nt(a, b):
    return lax.dot_general(a, b, (((1,), (1,)), ((), ())), preferred_element_type=F32)


def _rmsnorm_kernel(x_ref, g_ref, o_ref):
    o_ref[...] = _rms(x_ref[...].astype(F32), g_ref[...]).astype(o_ref.dtype)


def rmsnorm(x, g, out_dtype, name, tm=256):
    m, d = x.shape
    tm = min(tm, m)
    return pl.pallas_call(
        _rmsnorm_kernel,
        grid=(m // tm,),
        in_specs=[pl.BlockSpec((tm, d), lambda i: (i, 0)),
                  pl.BlockSpec((1, d), lambda i: (0, 0))],
        out_specs=pl.BlockSpec((tm, d), lambda i: (i, 0)),
        out_shape=jax.ShapeDtypeStruct((m, d), out_dtype),
        compiler_params=_params(("parallel",)),
        name=name,
    )(x, g.reshape(1, d).astype(F32))


def _matmul_kernel(a_ref, b_ref, o_ref):
    o_ref[...] = jnp.dot(a_ref[...], b_ref[...], preferred_element_type=F32).astype(o_ref.dtype)


def matmul(a, b, layer, out_dtype, name, tm=1024, tn=1024):
    m, k = a.shape
    n = b.shape[2]
    tm, tn = min(tm, m), min(tn, n)
    return pl.pallas_call(
        _matmul_kernel,
        grid=(m // tm, n // tn),
        in_specs=[pl.BlockSpec((tm, k), lambda i, j: (i, 0)),
                  pl.BlockSpec((None, k, tn), lambda i, j: (layer, 0, j))],
        out_specs=pl.BlockSpec((tm, tn), lambda i, j: (i, j)),
        out_shape=jax.ShapeDtypeStruct((m, n), out_dtype),
        compiler_params=_params(("parallel", "arbitrary")),
        name=name,
    )(a, b)


def _rope(u, ct, st):
    return u * ct + pltpu.roll(u, 64, 1) * st


def _qb_kernel(cq_ref, g_ref, w_ref, ct_ref, st_ref, o_ref, cqn_ref, *, heads, scale):
    @pl.when(pl.program_id(1) == 0)
    def _():
        cqn_ref[...] = _rms(cq_ref[...].astype(F32), g_ref[...]).astype(BF16)

    acc = jnp.dot(cqn_ref[...], w_ref[...], preferred_element_type=F32)
    ct = ct_ref[...] * scale
    st = st_ref[...] * scale
    for h in range(heads):
        lo = h * A_QPAD
        o_ref[:, lo:lo + A_NOPE] = (acc[:, lo:lo + A_NOPE] * scale).astype(BF16)
        o_ref[:, lo + A_NOPE:lo + A_QPAD] = _rope(acc[:, lo + A_NOPE:lo + A_QPAD], ct, st).astype(BF16)


def mla_q(proj, g, w_qb, layer, ct, st, cq_blk, tm=1024, heads=4):
    t = proj.shape[0]
    tm = min(tm, t)
    n = A_HEADS * A_QPAD
    tn = heads * A_QPAD
    scale = (A_NOPE + A_ROPE) ** -0.5 * math.log2(math.e)
    return pl.pallas_call(
        functools.partial(_qb_kernel, heads=heads, scale=scale),
        grid=(t // tm, n // tn),
        in_specs=[pl.BlockSpec((tm, A_Q_RANK), lambda i, j: (i, cq_blk)),
                  pl.BlockSpec((1, A_Q_RANK), lambda i, j: (0, 0)),
                  pl.BlockSpec((None, A_Q_RANK, tn), lambda i, j: (layer, 0, j)),
                  pl.BlockSpec((tm, LANES), lambda i, j: (i, 0)),
                  pl.BlockSpec((tm, LANES), lambda i, j: (i, 0))],
        out_specs=pl.BlockSpec((tm, tn), lambda i, j: (i, j)),
        out_shape=jax.ShapeDtypeStruct((t, n), BF16),
        scratch_shapes=[pltpu.VMEM((tm, A_Q_RANK), BF16)],
        compiler_params=_params(("parallel", "arbitrary")),
        name="mla_q_up",
    )(proj, g.reshape(1, -1).astype(F32), w_qb, ct, st)


def _kvb_kernel(ckv_ref, g_ref, wk_ref, wvt_ref, kr_ref, ct_ref, st_ref, kn_ref, vt_ref, kro_ref, cn_ref):
    @pl.when(pl.program_id(1) == 0)
    def _():
        cn_ref[...] = _rms(ckv_ref[...].astype(F32), g_ref[...]).astype(BF16)
        kro_ref[...] = _rope(kr_ref[...].astype(F32), ct_ref[...], st_ref[...]).astype(BF16)

    cn = cn_ref[...]
    kn_ref[...] = jnp.dot(cn, wk_ref[...], preferred_element_type=F32).astype(BF16)
    vt_ref[...] = _dot_nt(wvt_ref[...], cn).astype(BF16)


def mla_kv(proj, g, wk, wvt, layer, ct, st, ckv_blk, kr_blk, tm=1024, tn=1024):
    t = proj.shape[0]
    tm = min(tm, t)
    n = wk.shape[2]
    return pl.pallas_call(
        _kvb_kernel,
        grid=(t // tm, n // tn),
        in_specs=[pl.BlockSpec((tm, A_KV_RANK), lambda i, j: (i, ckv_blk)),
                  pl.BlockSpec((1, A_KV_RANK), lambda i, j: (0, 0)),
                  pl.BlockSpec((None, A_KV_RANK, tn), lambda i, j: (layer, 0, j)),
                  pl.BlockSpec((None, tn, A_KV_RANK), lambda i, j: (layer, j, 0)),
                  pl.BlockSpec((tm, LANES), lambda i, j: (i, kr_blk)),
                  pl.BlockSpec((tm, LANES), lambda i, j: (i, 0)),
                  pl.BlockSpec((tm, LANES), lambda i, j: (i, 0))],
        out_specs=[pl.BlockSpec((tm, tn), lambda i, j: (i, j)),
                   pl.BlockSpec((tn, tm), lambda i, j: (j, i)),
                   pl.BlockSpec((tm, LANES), lambda i, j: (i, 0))],
        out_shape=[jax.ShapeDtypeStruct((t, n), BF16),
                   jax.ShapeDtypeStruct((n, t), BF16),
                   jax.ShapeDtypeStruct((t, LANES), BF16)],
        scratch_shapes=[pltpu.VMEM((tm, A_KV_RANK), BF16)],
        compiler_params=_params(("parallel", "arbitrary")),
        name="mla_kv_up",
    )(proj, g.reshape(1, -1).astype(F32), wk, wvt, proj, ct, st)


def _mla_attn_kernel(q_ref, z_ref, kn_ref, kr_ref, vt_ref, o_ref, qt_ref, sa_ref, sb_ref, m_ref, acc_ref, *,
                     tq, heads):
    qi = pl.program_id(2)
    for h in range(heads):
        qt_ref[h] = q_ref[:, h * A_QPAD:(h + 1) * A_QPAD].astype(F32).T.astype(BF16)
    m_ref[...] = jnp.full(m_ref.shape, NEG, F32)
    acc_ref[...] = jnp.zeros(acc_ref.shape, F32)
    ones = jnp.ones((SUM_ROWS, tq), BF16)

    def scores(j, s_ref):
        rows = pl.ds(pl.multiple_of(j * tq, tq), tq)
        kr = kr_ref[rows, :]
        for h in range(heads):
            k = jnp.concatenate([kn_ref[rows, h * A_NOPE:(h + 1) * A_NOPE], kr], axis=1)
            s_ref[h] = jnp.dot(k, qt_ref[h], preferred_element_type=F32)

    def update(j, s_ref, masked):
        rows = pl.ds(pl.multiple_of(j * tq, tq), tq)
        for h in range(heads):
            s = s_ref[h]
            if masked:
                r = lax.broadcasted_iota(jnp.int32, s.shape, 0)
                c = lax.broadcasted_iota(jnp.int32, s.shape, 1)
                s = jnp.where(r <= c, s, NEG)
            m_prev = m_ref[h]
            m_cur = jnp.maximum(m_prev, jnp.max(s, axis=0, keepdims=True))
            alpha = jnp.exp2(m_prev - m_cur)
            p = jnp.exp2(s - m_cur).astype(BF16)
            va = jnp.concatenate([vt_ref[h * A_VDIM:(h + 1) * A_VDIM, rows], ones], axis=0)
            acc_ref[h] = alpha * acc_ref[h] + jnp.dot(va, p, preferred_element_type=F32)
            m_ref[h] = m_cur

    even = (qi % 2) == 1

    @pl.when(even)
    def _():
        scores(0, sb_ref)
        scores(1, sa_ref)
        update(0, sb_ref, False)

    @pl.when(jnp.logical_not(even))
    def _():
        scores(0, sa_ref)

    first = even.astype(jnp.int32)

    def pair(t, carry):
        c = first + 2 * t
        scores(c + 1, sb_ref)
        update(c, sa_ref, False)
        scores(c + 2, sa_ref)
        update(c + 1, sb_ref, False)
        return carry

    lax.fori_loop(0, (qi - first) // 2, pair, 0)
    update(qi, sa_ref, True)
    for h in range(heads):
        acc = acc_ref[h]
        o = (acc[:A_VDIM] * (1.0 / acc[A_VDIM:A_VDIM + 1])).T
        gate = _silu(z_ref[:, h * A_VDIM:(h + 1) * A_VDIM].astype(F32))
        o_ref[:, h * A_VDIM:(h + 1) * A_VDIM] = (o * gate).astype(o_ref.dtype)


def mla_attention(q, proj, kn, vt, kr, batch, seq, tq=512, heads=4):
    t = q.shape[0]
    tq = min(tq, seq)
    nq = seq // tq
    return pl.pallas_call(
        functools.partial(_mla_attn_kernel, tq=tq, heads=heads),
        grid=(batch, A_HEADS // heads, nq),
        in_specs=[pl.BlockSpec((tq, heads * A_QPAD), lambda b, h, i: (b * nq + i, h)),
                  pl.BlockSpec((tq, heads * A_VDIM), lambda b, h, i: (b * nq + i, h)),
                  pl.BlockSpec((seq, heads * A_NOPE), lambda b, h, i: (b, h)),
                  pl.BlockSpec((seq, LANES), lambda b, h, i: (b, 0)),
                  pl.BlockSpec((heads * A_VDIM, seq), lambda b, h, i: (h, b))],
        out_specs=pl.BlockSpec((tq, heads * A_VDIM), lambda b, h, i: (b * nq + i, h)),
        out_shape=jax.ShapeDtypeStruct((t, A_HEADS * A_VDIM), BF16),
        scratch_shapes=[pltpu.VMEM((heads, A_QPAD, tq), BF16),
                        pltpu.VMEM((heads, tq, tq), F32), pltpu.VMEM((heads, tq, tq), F32),
                        pltpu.VMEM((heads, 1, tq), F32),
                        pltpu.VMEM((heads, A_VDIM + SUM_ROWS, tq), F32)],
        compiler_params=_params(("parallel", "parallel", "arbitrary")),
        name="mla_attention",
    )(q, proj, kn, kr, vt)


def _swa_kernel(q_ref, z0_ref, z1_ref, z2_ref, kc_ref, kp_ref, vc_ref, vp_ref, bias_ref, sink_ref, o_ref,
                kd_ref, vf_ref, vt_ref, s_ref, p_ref, inv_ref, *, tq):
    first = pl.program_id(2) == 0
    r = tq // BLOCK
    low = lax.broadcasted_iota(jnp.int32, (BLOCK, LANES), 1) < B_HEAD_DIM
    for lo, src in ((0, kp_ref), (BLOCK, kc_ref)):
        k = src[...].astype(F32)
        kr = pltpu.roll(k, B_HEAD_DIM, 1)
        half = lax.broadcasted_iota(jnp.int32, k.shape, 1) < B_HEAD_DIM
        kd_ref[0, lo:lo + k.shape[0], :] = jnp.where(half, k, kr).astype(BF16)
        kd_ref[1, lo:lo + k.shape[0], :] = jnp.where(half, kr, k).astype(BF16)
    vf_ref[0:BLOCK, :] = vp_ref[...]
    vf_ref[BLOCK:, :] = vc_ref[...]
    for c in range(r):
        vt = vf_ref[c * BLOCK:(c + 2) * BLOCK, :].astype(F32).T
        for e in range(2):
            ve = vt[e * B_HEAD_DIM:(e + 1) * B_HEAD_DIM]
            vt_ref[e, c] = jnp.concatenate([ve, ve], axis=0).astype(BF16)
    key = lax.broadcasted_iota(jnp.int32, (2 * BLOCK, 2 * BLOCK), 0)
    pen0 = jnp.where((key < BLOCK) & first, NEG, 0.0).astype(F32)
    tiles = [(c, l) for c in range(r) for l in range(B_GROUP)]
    for t, (c, l) in enumerate(tiles):
        qp = q_ref[c * BLOCK:(c + 1) * BLOCK, l * LANES:(l + 1) * LANES]
        zero = jnp.zeros_like(qp)
        q2 = jnp.concatenate([jnp.where(low, qp, zero), jnp.where(low, zero, qp)], axis=0)
        s = _dot_nt(kd_ref[l // 3, c * BLOCK:(c + 2) * BLOCK, :], q2) + bias_ref[0, l]
        s_ref[t] = s + pen0 if c == 0 else s
    for t, (c, l) in enumerate(tiles):
        s = s_ref[t]
        sink = sink_ref[0, l]
        m = jnp.maximum(jnp.max(s, axis=0, keepdims=True), sink)
        p = jnp.exp(s - m)
        inv_ref[t] = 1.0 / (jnp.sum(p, axis=0, keepdims=True) + jnp.exp(sink - m))
        p_ref[t] = p.astype(BF16)
    z_refs = (z0_ref, z1_ref, z2_ref)
    for t, (c, l) in enumerate(tiles):
        rows = slice(c * BLOCK, (c + 1) * BLOCK)
        ot = jnp.dot(vt_ref[l // 3, c], p_ref[t], preferred_element_type=F32) * inv_ref[t]
        o = jnp.concatenate([ot[0:B_HEAD_DIM, 0:BLOCK], ot[B_HEAD_DIM:, BLOCK:]], axis=0).T
        z = z_refs[l // 2][rows, (l % 2) * LANES:(l % 2 + 1) * LANES].astype(F32)
        o_ref[rows, l * LANES:(l + 1) * LANES] = (o * _silu(z)).astype(o_ref.dtype)


def swa_attention(proj, bias, sink, batch, seq, tq=512):
    t = proj.shape[0]
    tq = min(tq, seq)
    nq = seq // tq
    r = tq // BLOCK
    qw = B_GROUP * LANES
    k_off, v_off = 3072 // LANES, 3584 // LANES
    z_off = 5120 // (2 * LANES)

    def prev_map(p, b, i, off):
        return (jnp.maximum(b * (seq // BLOCK) + i * r - 1, b * (seq // BLOCK)), off + p)

    def z_map(p, b, i, part):
        return (b * nq + i, z_off + 3 * p + part)

    return pl.pallas_call(
        functools.partial(_swa_kernel, tq=tq),
        grid=(B_PAIRS, batch, nq),
        in_specs=[pl.BlockSpec((tq, qw), lambda p, b, i: (b * nq + i, p)),
                  pl.BlockSpec((tq, 2 * LANES), functools.partial(z_map, part=0)),
                  pl.BlockSpec((tq, 2 * LANES), functools.partial(z_map, part=1)),
                  pl.BlockSpec((tq, 2 * LANES), functools.partial(z_map, part=2)),
                  pl.BlockSpec((tq, LANES), lambda p, b, i: (b * nq + i, k_off + p)),
                  pl.BlockSpec((BLOCK, LANES), functools.partial(prev_map, off=k_off)),
                  pl.BlockSpec((tq, LANES), lambda p, b, i: (b * nq + i, v_off + p)),
                  pl.BlockSpec((BLOCK, LANES), functools.partial(prev_map, off=v_off)),
                  pl.BlockSpec((1, B_GROUP, 2 * BLOCK, 2 * BLOCK), lambda p, b, i: (p, 0, 0, 0)),
                  pl.BlockSpec((1, B_GROUP, 1, 2 * BLOCK), lambda p, b, i: (p, 0, 0, 0))],
        out_specs=pl.BlockSpec((tq, qw), lambda p, b, i: (b * nq + i, p)),
        out_shape=jax.ShapeDtypeStruct((t, SELF_WIDTH), BF16),
        scratch_shapes=[pltpu.VMEM((2, tq + BLOCK, LANES), BF16), pltpu.VMEM((tq + BLOCK, LANES), BF16),
                        pltpu.VMEM((2, r, LANES, 2 * BLOCK), BF16),
                        pltpu.VMEM((r * B_GROUP, 2 * BLOCK, 2 * BLOCK), F32),
                        pltpu.VMEM((r * B_GROUP, 2 * BLOCK, 2 * BLOCK), BF16),
                        pltpu.VMEM((r * B_GROUP, 1, 2 * BLOCK), F32)],
        compiler_params=_params(("parallel", "parallel", "arbitrary")),
        name="swa_attention",
    )(proj, proj, proj, proj, proj, proj, proj, proj, bias, sink)


def _memattn_kernel(xq_ref, z_ref, mk_ref, mv_ref, o_ref):
    for h in range(X_HEADS):
        cols = slice(h * X_HEAD_DIM, (h + 1) * X_HEAD_DIM)
        s = _dot_nt(xq_ref[:, cols], mk_ref[:, cols])
        m = jnp.max(s, axis=1, keepdims=True)
        p = jnp.exp(s - m)
        l = jnp.sum(p, axis=1, keepdims=True)
        o = jnp.dot(p.astype(BF16), mv_ref[:, cols], preferred_element_type=F32) / l
        o_ref[:, cols] = (o * _silu(z_ref[:, cols].astype(F32))).astype(o_ref.dtype)


def mem_attention(proj, mkv, batch, seq, xq_blk, z_blk, tq=512):
    t = proj.shape[0]
    n_mem = mkv.shape[0] // batch
    tq = min(tq, seq)
    nq = seq // tq
    return pl.pallas_call(
        _memattn_kernel,
        grid=(batch, nq),
        in_specs=[pl.BlockSpec((tq, X_WIDTH), lambda b, i: (b * nq + i, xq_blk)),
                  pl.BlockSpec((tq, X_WIDTH), lambda b, i: (b * nq + i, z_blk)),
                  pl.BlockSpec((n_mem, X_WIDTH), lambda b, i: (b, 0)),
                  pl.BlockSpec((n_mem, X_WIDTH), lambda b, i: (b, 1))],
        out_specs=pl.BlockSpec((tq, X_WIDTH), lambda b, i: (b * nq + i, 0)),
        out_shape=jax.ShapeDtypeStruct((t, X_WIDTH), BF16),
        compiler_params=_params(("parallel", "arbitrary")),
        name="mem_attention",
    )(proj, proj, mkv, mkv)


def _outproj_kernel(ys_ref, ym_ref, x_ref, ws_ref, wm_ref, o_ref):
    acc = jnp.dot(ys_ref[...], ws_ref[...], preferred_element_type=F32)
    acc = acc + jnp.dot(ym_ref[...], wm_ref[...], preferred_element_type=F32)
    o_ref[...] = x_ref[...] + acc


def out_proj(y_self, y_mem, x, w_out, layer, tm=1024, tn=512):
    t, d = x.shape
    tm = min(tm, t)
    return pl.pallas_call(
        _outproj_kernel,
        grid=(t // tm, d // tn),
        in_specs=[pl.BlockSpec((tm, SELF_WIDTH), lambda i, j: (i, 0)),
                  pl.BlockSpec((tm, X_WIDTH), lambda i, j: (i, 0)),
                  pl.BlockSpec((tm, tn), lambda i, j: (i, j)),
                  pl.BlockSpec((None, SELF_WIDTH, tn), lambda i, j: (layer, 0, j)),
                  pl.BlockSpec((None, X_WIDTH, tn), lambda i, j: (layer, SELF_WIDTH // X_WIDTH, j))],
        out_specs=pl.BlockSpec((tm, tn), lambda i, j: (i, j)),
        out_shape=jax.ShapeDtypeStruct((t, d), F32),
        compiler_params=_params(("parallel", "arbitrary")),
        name="out_proj",
    )(y_self, y_mem, x, w_out, w_out)


A_IN_PAD = 7168


def _relayout_a_in_kernel(w_ref, o_ref):
    o1, o2, o3 = A_Q_RANK, A_Q_RANK + A_KV_RANK, A_Q_RANK + A_KV_RANK + A_ROPE
    o4 = o3 + X_WIDTH
    x = w_ref[0]
    o_ref[0, :, 0:4096] = x[:, o4:].astype(BF16)
    o_ref[0, :, 4096:5120] = x[:, :o1].astype(BF16)
    o_ref[0, :, 5120:6144] = (x[:, o3:o4] * (X_HEAD_DIM ** -0.5)).astype(BF16)
    o_ref[0, :, 6144:6656] = x[:, o1:o2].astype(BF16)
    u = x[:, o2:o2 + LANES]
    lane = lax.broadcasted_iota(jnp.int32, u.shape, 1)
    kr = jnp.where(lane < 64, u, jnp.where(lane < 96, pltpu.roll(u, 32, 1), pltpu.roll(u, 96, 1)))
    o_ref[0, :, 6656:6784] = kr.astype(BF16)
    o_ref[0, :, 6784:] = jnp.zeros((x.shape[0], A_IN_PAD - 6784), BF16)


def relayout_a_in(w, tr=256):
    n_layers, k, n = w.shape
    return pl.pallas_call(
        _relayout_a_in_kernel,
        grid=(n_layers, k // tr),
        in_specs=[pl.BlockSpec((1, tr, n), lambda a, i: (a, i, 0))],
        out_specs=pl.BlockSpec((1, tr, A_IN_PAD), lambda a, i: (a, i, 0)),
        out_shape=jax.ShapeDtypeStruct((n_layers, k, A_IN_PAD), BF16),
        compiler_params=_params(("parallel", "parallel")),
        name="relayout_a_w_in",
    )(w)


def _rope_tables(positions):
    inv = ROPE_THETA ** (-jnp.arange(0, A_ROPE, 2, dtype=F32) / A_ROPE)
    ang = positions.astype(F32).reshape(-1)[:, None] * inv
    cos, sin = jnp.cos(ang), jnp.sin(ang)
    zero = jnp.zeros_like(cos)
    ct = jnp.concatenate([cos, cos, zero, zero], axis=1)
    st = jnp.concatenate([-sin, sin, zero, zero], axis=1)
    return ct, st


def _swap_halves(w):
    half = w.shape[-1] // 2
    return jnp.concatenate([w[..., half:], w[..., :half]], axis=-1)


def _prep_a_w_qb(w):
    n_layers = w.shape[0]
    w = w.astype(BF16).reshape(n_layers, A_Q_RANK, A_HEADS, A_NOPE + A_ROPE)
    rope = w[..., A_NOPE:]
    out = jnp.concatenate([w[..., :A_NOPE], rope, _swap_halves(rope)], axis=-1)
    return out.reshape(n_layers, A_Q_RANK, A_HEADS * A_QPAD)


def _prep_a_w_kvb(w):
    n_layers = w.shape[0]
    w = w.astype(BF16).reshape(n_layers, A_KV_RANK, A_HEADS, A_NOPE + A_VDIM)
    wk = w[..., :A_NOPE].reshape(n_layers, A_KV_RANK, A_HEADS * A_NOPE)
    wvt = w[..., A_NOPE:].reshape(n_layers, A_KV_RANK, A_HEADS * A_VDIM).swapaxes(1, 2)
    return wk, wvt


def _prep_b_w_in(w):
    nq = B_HEADS * B_HEAD_DIM
    nk = B_KV_HEADS * B_HEAD_DIM
    col = jnp.arange(w.shape[-1])
    scale = jnp.where(col < nq, B_HEAD_DIM ** -0.5,
                      jnp.where((col >= nq + 2 * nk) & (col < nq + 2 * nk + X_WIDTH), X_HEAD_DIM ** -0.5, 1.0))
    return (w * scale.astype(F32)).astype(BF16)


def _t5_bucket(dist):
    n = jnp.maximum(dist, 0)
    nf = jnp.maximum(n, 1).astype(F32)
    large = MAX_EXACT + (jnp.log(nf / MAX_EXACT) / math.log(MAX_DIST / MAX_EXACT)
                         * (N_BUCKETS - MAX_EXACT)).astype(jnp.int32)
    large = jnp.minimum(large, N_BUCKETS - 1)
    return jnp.where(n < MAX_EXACT, n, large)


def _swa_tables(rel_bias, sinks):
    q_local = jnp.arange(BLOCK)[None, :]
    k_local = jnp.arange(2 * BLOCK)[:, None]
    dist = q_local + BLOCK - k_local
    in_window = (dist >= 0) & (dist < WINDOW)
    onehot = (_t5_bucket(dist)[:, :, None] == jnp.arange(N_BUCKETS)).astype(F32)
    bias = jnp.einsum("kqn,nh->kqh", onehot, rel_bias.astype(F32), precision=lax.Precision.HIGHEST)
    bias = jnp.where(in_window[:, :, None], bias, NEG)
    bias = bias.reshape(2 * BLOCK, BLOCK, B_PAIRS, B_GROUP, 2)
    bias = bias.transpose(2, 3, 0, 4, 1).reshape(B_PAIRS, B_GROUP, 2 * BLOCK, 2 * BLOCK)
    sink = sinks.astype(F32).reshape(B_PAIRS, B_GROUP, 2, 1)
    sink = jnp.broadcast_to(sink, (B_PAIRS, B_GROUP, 2, BLOCK)).reshape(B_PAIRS, B_GROUP, 1, 2 * BLOCK)
    return bias, sink


def kernel(x, mem, positions, norm_g, mem_norm_g, final_norm_g, w_mem_kv, w_out, a_w_in, a_q_norm_g, a_kv_norm_g,
           a_w_qb, a_w_kvb, b_w_in, b_sinks, rel_bias):
    batch, seq, d = x.shape
    n_mem = mem.shape[1]
    depth = norm_g.shape[0]
    xs = x.reshape(batch * seq, d)
    mems = mem.reshape(batch * n_mem, d)
    ct, st = _rope_tables(positions)
    w_mem_kv_b = w_mem_kv.astype(BF16)
    w_out_b = w_out.astype(BF16)
    a_w_in_b = relayout_a_in(a_w_in)
    b_w_in_b = _prep_b_w_in(b_w_in)
    a_w_qb_b = _prep_a_w_qb(a_w_qb)
    a_wk_b, a_wvt_b = _prep_a_w_kvb(a_w_kvb)

    for i in range(depth):
        j = i // 2
        h = rmsnorm(xs, norm_g[i], BF16, "x_rmsnorm")
        mn = rmsnorm(mems, mem_norm_g[i], BF16, "mem_rmsnorm")
        mkv = matmul(mn, w_mem_kv_b, i, BF16, "mem_kv_proj")
        if i % 2 == 0:
            proj = matmul(h, a_w_in_b, j, BF16, "mla_in_proj")
            q = mla_q(proj, a_q_norm_g[j], a_w_qb_b, j, ct, st, cq_blk=4)
            kn, vt, kr = mla_kv(proj, a_kv_norm_g[j], a_wk_b, a_wvt_b, j, ct, st, ckv_blk=12, kr_blk=52)
            y_self = mla_attention(q, proj, kn, vt, kr, batch, seq)
            y_mem = mem_attention(proj, mkv, batch, seq, xq_blk=5, z_blk=3)
        else:
            proj = matmul(h, b_w_in_b, j, BF16, "swa_in_proj")
            bias, sink = _swa_tables(rel_bias, b_sinks[j])
            y_self = swa_attention(proj, bias, sink, batch, seq)
            y_mem = mem_attention(proj, mkv, batch, seq, xq_blk=4, z_blk=8)
        xs = out_proj(y_self, y_mem, xs, w_out_b, i)
    out = rmsnorm(xs, final_norm_g, x.dtype, "final_rmsnorm")
    return out.reshape(batch, seq, d)
```

```python
import functools
import math

import jax
import jax.numpy as jnp
from jax import lax
from jax.experimental import pallas as pl
from jax.experimental.pallas import tpu as pltpu

F32 = jnp.float32
BF16 = jnp.bfloat16

EPS = 1e-6
LANES = 128
NEG = -1e30
VMEM_LIMIT = 56 * 1024 * 1024

X_HEADS = 4
X_HEAD_DIM = 256
X_WIDTH = 1024
SELF_WIDTH = 3072
A_NOPE = 128
A_ROPE = 64
A_VDIM = 128
A_HEADS = 24
A_Q_RANK = 1024
A_KV_RANK = 512
SUM_ROWS = 16
A_QPAD = 256
ROPE_THETA = 10000.0
B_HEAD_DIM = 64
B_HEADS = 48
B_KV_HEADS = 8
B_GROUP = 6
B_PAIRS = B_KV_HEADS // 2
WINDOW = 128
BLOCK = 128
N_BUCKETS = 32
MAX_EXACT = 16
MAX_DIST = 128


def _params(sem):
    return pltpu.CompilerParams(dimension_semantics=sem, vmem_limit_bytes=VMEM_LIMIT)


def _rms(x, g):
    ms = jnp.mean(x * x, axis=-1, keepdims=True)
    return x * lax.rsqrt(ms + EPS) * g


def _silu(z):
    return z * jax.nn.sigmoid(z)


def _dot_nt(a, b):
    return lax.dot_general(a, b, (((1,), (1,)), ((), ())), preferred_element_type=F32)


def _rmsnorm_kernel(x_ref, g_ref, o_ref):
    o_ref[...] = _rms(x_ref[...].astype(F32), g_ref[...]).astype(o_ref.dtype)


def rmsnorm(x, g, out_dtype, name, tm=256):
    m, d = x.shape
    tm = min(tm, m)
    return pl.pallas_call(
        _rmsnorm_kernel,
        grid=(m // tm,),
        in_specs=[pl.BlockSpec((tm, d), lambda i: (i, 0)),
                  pl.BlockSpec((1, d), lambda i: (0, 0))],
        out_specs=pl.BlockSpec((tm, d), lambda i: (i, 0)),
        out_shape=jax.ShapeDtypeStruct((m, d), out_dtype),
        compiler_params=_params(("parallel",)),
        name=name,
    )(x, g.reshape(1, d).astype(F32))


def _matmul_kernel(a_ref, b_ref, o_ref):
    o_ref[...] = jnp.dot(a_ref[...], b_ref[...], preferred_element_type=F32).astype(o_ref.dtype)


def matmul(a, b, layer, out_dtype, name, tm=1024, tn=1024):
    m, k = a.shape
    n = b.shape[2]
    tm, tn = min(tm, m), min(tn, n)
    return pl.pallas_call(
        _matmul_kernel,
        grid=(m // tm, n // tn),
        in_specs=[pl.BlockSpec((tm, k), lambda i, j: (i, 0)),
                  pl.BlockSpec((None, k, tn), lambda i, j: (layer, 0, j))],
        out_specs=pl.BlockSpec((tm, tn), lambda i, j: (i, j)),
        out_shape=jax.ShapeDtypeStruct((m, n), out_dtype),
        compiler_params=_params(("parallel", "arbitrary")),
        name=name,
    )(a, b)


def _cast_sumsq_kernel(x_ref, xb_ref, ss_ref):
    x = x_ref[...]
    xb_ref[...] = x.astype(BF16)
    ss_ref[...] = jnp.sum(x * x, axis=-1, keepdims=True)


def cast_sumsq(x, tm=256):
    m, d = x.shape
    tm = min(tm, m)
    return pl.pallas_call(
        _cast_sumsq_kernel,
        grid=(m // tm,),
        in_specs=[pl.BlockSpec((tm, d), lambda i: (i, 0))],
        out_specs=[pl.BlockSpec((tm, d), lambda i: (i, 0)), pl.BlockSpec((tm, 1), lambda i: (i, 0))],
        out_shape=[jax.ShapeDtypeStruct((m, d), BF16), jax.ShapeDtypeStruct((m, 1), F32)],
        compiler_params=_params(("parallel",)),
        name="cast_sumsq",
    )(x)


def _norm_matmul_kernel(a_ref, ss_ref, b_ref, o_ref, *, width):
    r = lax.rsqrt(ss_ref[...] * (1.0 / width) + EPS)
    o_ref[...] = (jnp.dot(a_ref[...], b_ref[...], preferred_element_type=F32) * r).astype(o_ref.dtype)


def norm_matmul(a, ss, b, layer, name, tm=1024, tn=1024):
    m, k = a.shape
    n = b.shape[2]
    tm, tn = min(tm, m), min(tn, n)
    return pl.pallas_call(
        functools.partial(_norm_matmul_kernel, width=k),
        grid=(m // tm, n // tn),
        in_specs=[pl.BlockSpec((tm, k), lambda i, j: (i, 0)),
                  pl.BlockSpec((tm, 1), lambda i, j: (i, 0)),
                  pl.BlockSpec((None, k, tn), lambda i, j: (layer, 0, j))],
        out_specs=pl.BlockSpec((tm, tn), lambda i, j: (i, j)),
        out_shape=jax.ShapeDtypeStruct((m, n), BF16),
        compiler_params=_params(("parallel", "arbitrary")),
        name=name,
    )(a, ss, b)


def _scale_rows_kernel(x_ref, ss_ref, g_ref, o_ref):
    x = x_ref[...]
    r = lax.rsqrt(ss_ref[...] * (1.0 / x.shape[-1]) + EPS)
    o_ref[...] = (x * r * g_ref[...]).astype(o_ref.dtype)


def rmsnorm_from_sumsq(x, ss, g, tm=256):
    m, d = x.shape
    tm = min(tm, m)
    return pl.pallas_call(
        _scale_rows_kernel,
        grid=(m // tm,),
        in_specs=[pl.BlockSpec((tm, d), lambda i: (i, 0)),
                  pl.BlockSpec((tm, 1), lambda i: (i, 0)),
                  pl.BlockSpec((1, d), lambda i: (0, 0))],
        out_specs=pl.BlockSpec((tm, d), lambda i: (i, 0)),
        out_shape=jax.ShapeDtypeStruct((m, d), x.dtype),
        compiler_params=_params(("parallel",)),
        name="final_rmsnorm",
    )(x, ss, g.reshape(1, d).astype(F32))


def _rope(u, ct, st):
    return u * ct + pltpu.roll(u, 64, 1) * st


def _qb_kernel(cq_ref, g_ref, w_ref, ct_ref, st_ref, o_ref, cqn_ref, *, heads, scale):
    @pl.when(pl.program_id(1) == 0)
    def _():
        cqn_ref[...] = _rms(cq_ref[...].astype(F32), g_ref[...]).astype(BF16)

    acc = jnp.dot(cqn_ref[...], w_ref[...], preferred_element_type=F32)
    ct = ct_ref[...] * scale
    st = st_ref[...] * scale
    for h in range(heads):
        lo = h * A_QPAD
        o_ref[:, lo:lo + A_NOPE] = (acc[:, lo:lo + A_NOPE] * scale).astype(BF16)
        o_ref[:, lo + A_NOPE:lo + A_QPAD] = _rope(acc[:, lo + A_NOPE:lo + A_QPAD], ct, st).astype(BF16)


def mla_q(proj, g, w_qb, layer, ct, st, cq_blk, tm=1024, heads=4):
    t = proj.shape[0]
    tm = min(tm, t)
    n = A_HEADS * A_QPAD
    tn = heads * A_QPAD
    scale = (A_NOPE + A_ROPE) ** -0.5 * math.log2(math.e)
    return pl.pallas_call(
        functools.partial(_qb_kernel, heads=heads, scale=scale),
        grid=(t // tm, n // tn),
        in_specs=[pl.BlockSpec((tm, A_Q_RANK), lambda i, j: (i, cq_blk)),
                  pl.BlockSpec((1, A_Q_RANK), lambda i, j: (0, 0)),
                  pl.BlockSpec((None, A_Q_RANK, tn), lambda i, j: (layer, 0, j)),
                  pl.BlockSpec((tm, LANES), lambda i, j: (i, 0)),
                  pl.BlockSpec((tm, LANES), lambda i, j: (i, 0))],
        out_specs=pl.BlockSpec((tm, tn), lambda i, j: (i, j)),
        out_shape=jax.ShapeDtypeStruct((t, n), BF16),
        scratch_shapes=[pltpu.VMEM((tm, A_Q_RANK), BF16)],
        compiler_params=_params(("parallel", "arbitrary")),
        name="mla_q_up",
    )(proj, g.reshape(1, -1).astype(F32), w_qb, ct, st)


def _kvb_kernel(ckv_ref, g_ref, wk_ref, wvt_ref, kr_ref, ct_ref, st_ref, kn_ref, vt_ref, kro_ref, cn_ref):
    @pl.when(pl.program_id(1) == 0)
    def _():
        cn_ref[...] = _rms(ckv_ref[...].astype(F32), g_ref[...]).astype(BF16)
        kro_ref[...] = _rope(kr_ref[...].astype(F32), ct_ref[...], st_ref[...]).astype(BF16)

    cn = cn_ref[...]
    kn_ref[...] = jnp.dot(cn, wk_ref[...], preferred_element_type=F32).astype(BF16)
    vt_ref[...] = _dot_nt(wvt_ref[...], cn).astype(BF16)


def mla_kv(proj, g, wk, wvt, layer, ct, st, ckv_blk, kr_blk, tm=1024, tn=1024):
    t = proj.shape[0]
    tm = min(tm, t)
    n = wk.shape[2]
    return pl.pallas_call(
        _kvb_kernel,
        grid=(t // tm, n // tn),
        in_specs=[pl.BlockSpec((tm, A_KV_RANK), lambda i, j: (i, ckv_blk)),
                  pl.BlockSpec((1, A_KV_RANK), lambda i, j: (0, 0)),
                  pl.BlockSpec((None, A_KV_RANK, tn), lambda i, j: (layer, 0, j)),
                  pl.BlockSpec((None, tn, A_KV_RANK), lambda i, j: (layer, j, 0)),
                  pl.BlockSpec((tm, LANES), lambda i, j: (i, kr_blk)),
                  pl.BlockSpec((tm, LANES), lambda i, j: (i, 0)),
                  pl.BlockSpec((tm, LANES), lambda i, j: (i, 0))],
        out_specs=[pl.BlockSpec((tm, tn), lambda i, j: (i, j)),
                   pl.BlockSpec((tn, tm), lambda i, j: (j, i)),
                   pl.BlockSpec((tm, LANES), lambda i, j: (i, 0))],
        out_shape=[jax.ShapeDtypeStruct((t, n), BF16),
                   jax.ShapeDtypeStruct((n, t), BF16),
                   jax.ShapeDtypeStruct((t, LANES), BF16)],
        scratch_shapes=[pltpu.VMEM((tm, A_KV_RANK), BF16)],
        compiler_params=_params(("parallel", "arbitrary")),
        name="mla_kv_up",
    )(proj, g.reshape(1, -1).astype(F32), wk, wvt, proj, ct, st)


def _mla_attn_kernel(q_ref, z_ref, kn_ref, kr_ref, vt_ref, o_ref, qt_ref, sa_ref, sb_ref, m_ref, acc_ref, *,
                     tq, heads):
    qi = pl.program_id(2)
    for h in range(heads):
        qt_ref[h] = q_ref[:, h * A_QPAD:(h + 1) * A_QPAD].astype(F32).T.astype(BF16)
    m_ref[...] = jnp.full(m_ref.shape, NEG, F32)
    acc_ref[...] = jnp.zeros(acc_ref.shape, F32)
    ones = jnp.ones((SUM_ROWS, tq), BF16)

    def scores(j, s_ref):
        rows = pl.ds(pl.multiple_of(j * tq, tq), tq)
        kr = kr_ref[rows, :]
        for h in range(heads):
            k = jnp.concatenate([kn_ref[rows, h * A_NOPE:(h + 1) * A_NOPE], kr], axis=1)
            s_ref[h] = jnp.dot(k, qt_ref[h], preferred_element_type=F32)

    def update(j, s_ref, masked):
        rows = pl.ds(pl.multiple_of(j * tq, tq), tq)
        for h in range(heads):
            s = s_ref[h]
            if masked:
                r = lax.broadcasted_iota(jnp.int32, s.shape, 0)
                c = lax.broadcasted_iota(jnp.int32, s.shape, 1)
                s = jnp.where(r <= c, s, NEG)
            m_prev = m_ref[h]
            m_cur = jnp.maximum(m_prev, jnp.max(s, axis=0, keepdims=True))
            alpha = jnp.exp2(m_prev - m_cur)
            p = jnp.exp2(s - m_cur).astype(BF16)
            va = jnp.concatenate([vt_ref[h * A_VDIM:(h + 1) * A_VDIM, rows], ones], axis=0)
            acc_ref[h] = alpha * acc_ref[h] + jnp.dot(va, p, preferred_element_type=F32)
            m_ref[h] = m_cur

    even = (qi % 2) == 1

    @pl.when(even)
    def _():
        scores(0, sb_ref)
        scores(1, sa_ref)
        update(0, sb_ref, False)

    @pl.when(jnp.logical_not(even))
    def _():
        scores(0, sa_ref)

    first = even.astype(jnp.int32)

    def pair(t, carry):
        c = first + 2 * t
        scores(c + 1, sb_ref)
        update(c, sa_ref, False)
        scores(c + 2, sa_ref)
        update(c + 1, sb_ref, False)
        return carry

    lax.fori_loop(0, (qi - first) // 2, pair, 0)
    update(qi, sa_ref, True)
    for h in range(heads):
        acc = acc_ref[h]
        o = (acc[:A_VDIM] * (1.0 / acc[A_VDIM:A_VDIM + 1])).T
        gate = _silu(z_ref[:, h * A_VDIM:(h + 1) * A_VDIM].astype(F32))
        o_ref[:, h * A_VDIM:(h + 1) * A_VDIM] = (o * gate).astype(o_ref.dtype)


def mla_attention(q, proj, kn, vt, kr, batch, seq, tq=512, heads=4):
    t = q.shape[0]
    tq = min(tq, seq)
    nq = seq // tq
    return pl.pallas_call(
        functools.partial(_mla_attn_kernel, tq=tq, heads=heads),
        grid=(batch, A_HEADS // heads, nq),
        in_specs=[pl.BlockSpec((tq, heads * A_QPAD), lambda b, h, i: (b * nq + i, h)),
                  pl.BlockSpec((tq, heads * A_VDIM), lambda b, h, i: (b * nq + i, h)),
                  pl.BlockSpec((seq, heads * A_NOPE), lambda b, h, i: (b, h)),
                  pl.BlockSpec((seq, LANES), lambda b, h, i: (b, 0)),
                  pl.BlockSpec((heads * A_VDIM, seq), lambda b, h, i: (h, b))],
        out_specs=pl.BlockSpec((tq, heads * A_VDIM), lambda b, h, i: (b * nq + i, h)),
        out_shape=jax.ShapeDtypeStruct((t, A_HEADS * A_VDIM), BF16),
        scratch_shapes=[pltpu.VMEM((heads, A_QPAD, tq), BF16),
                        pltpu.VMEM((heads, tq, tq), F32), pltpu.VMEM((heads, tq, tq), F32),
                        pltpu.VMEM((heads, 1, tq), F32),
                        pltpu.VMEM((heads, A_VDIM + SUM_ROWS, tq), F32)],
        compiler_params=_params(("parallel", "parallel", "arbitrary")),
        name="mla_attention",
    )(q, proj, kn, kr, vt)


def _swa_kernel(q_ref, z0_ref, z1_ref, z2_ref, kc_ref, kp_ref, vc_ref, vp_ref, bias_ref, sink_ref, o_ref,
                kd_ref, vf_ref, vt_ref, s_ref, p_ref, inv_ref, *, tq):
    first = pl.program_id(2) == 0
    r = tq // BLOCK
    low = lax.broadcasted_iota(jnp.int32, (BLOCK, LANES), 1) < B_HEAD_DIM
    for lo, src in ((0, kp_ref), (BLOCK, kc_ref)):
        k = src[...].astype(F32)
        kr = pltpu.roll(k, B_HEAD_DIM, 1)
        half = lax.broadcasted_iota(jnp.int32, k.shape, 1) < B_HEAD_DIM
        kd_ref[0, lo:lo + k.shape[0], :] = jnp.where(half, k, kr).astype(BF16)
        kd_ref[1, lo:lo + k.shape[0], :] = jnp.where(half, kr, k).astype(BF16)
    vf_ref[0:BLOCK, :] = vp_ref[...]
    vf_ref[BLOCK:, :] = vc_ref[...]
    for c in range(r):
        vt = vf_ref[c * BLOCK:(c + 2) * BLOCK, :].astype(F32).T
        for e in range(2):
            ve = vt[e * B_HEAD_DIM:(e + 1) * B_HEAD_DIM]
            vt_ref[e, c] = jnp.concatenate([ve, ve], axis=0).astype(BF16)
    key = lax.broadcasted_iota(jnp.int32, (2 * BLOCK, 2 * BLOCK), 0)
    pen0 = jnp.where((key < BLOCK) & first, NEG, 0.0).astype(F32)
    tiles = [(c, l) for c in range(r) for l in range(B_GROUP)]
    for t, (c, l) in enumerate(tiles):
        qp = q_ref[c * BLOCK:(c + 1) * BLOCK, l * LANES:(l + 1) * LANES]
        zero = jnp.zeros_like(qp)
        q2 = jnp.concatenate([jnp.where(low, qp, zero), jnp.where(low, zero, qp)], axis=0)
        s = _dot_nt(kd_ref[l // 3, c * BLOCK:(c + 2) * BLOCK, :], q2) + bias_ref[0, l]
        s_ref[t] = s + pen0 if c == 0 else s
    for t, (c, l) in enumerate(tiles):
        s = s_ref[t]
        sink = sink_ref[0, l]
        m = jnp.maximum(jnp.max(s, axis=0, keepdims=True), sink)
        p = jnp.exp(s - m)
        inv_ref[t] = 1.0 / (jnp.sum(p, axis=0, keepdims=True) + jnp.exp(sink - m))
        p_ref[t] = p.astype(BF16)
    z_refs = (z0_ref, z1_ref, z2_ref)
    for t, (c, l) in enumerate(tiles):
        rows = slice(c * BLOCK, (c + 1) * BLOCK)
        ot = jnp.dot(vt_ref[l // 3, c], p_ref[t], preferred_element_type=F32) * inv_ref[t]
        o = jnp.concatenate([ot[0:B_HEAD_DIM, 0:BLOCK], ot[B_HEAD_DIM:, BLOCK:]], axis=0).T
        z = z_refs[l // 2][rows, (l % 2) * LANES:(l % 2 + 1) * LANES].astype(F32)
        o_ref[rows, l * LANES:(l + 1) * LANES] = (o * _silu(z)).astype(o_ref.dtype)


def swa_attention(proj, bias, sink, batch, seq, tq=512):
    t = proj.shape[0]
    tq = min(tq, seq)
    nq = seq // tq
    r = tq // BLOCK
    qw = B_GROUP * LANES
    k_off, v_off = 3072 // LANES, 3584 // LANES
    z_off = 5120 // (2 * LANES)

    def prev_map(p, b, i, off):
        return (jnp.maximum(b * (seq // BLOCK) + i * r - 1, b * (seq // BLOCK)), off + p)

    def z_map(p, b, i, part):
        return (b * nq + i, z_off + 3 * p + part)

    return pl.pallas_call(
        functools.partial(_swa_kernel, tq=tq),
        grid=(B_PAIRS, batch, nq),
        in_specs=[pl.BlockSpec((tq, qw), lambda p, b, i: (b * nq + i, p)),
                  pl.BlockSpec((tq, 2 * LANES), functools.partial(z_map, part=0)),
                  pl.BlockSpec((tq, 2 * LANES), functools.partial(z_map, part=1)),
                  pl.BlockSpec((tq, 2 * LANES), functools.partial(z_map, part=2)),
                  pl.BlockSpec((tq, LANES), lambda p, b, i: (b * nq + i, k_off + p)),
                  pl.BlockSpec((BLOCK, LANES), functools.partial(prev_map, off=k_off)),
                  pl.BlockSpec((tq, LANES), lambda p, b, i: (b * nq + i, v_off + p)),
                  pl.BlockSpec((BLOCK, LANES), functools.partial(prev_map, off=v_off)),
                  pl.BlockSpec((1, B_GROUP, 2 * BLOCK, 2 * BLOCK), lambda p, b, i: (p, 0, 0, 0)),
                  pl.BlockSpec((1, B_GROUP, 1, 2 * BLOCK), lambda p, b, i: (p, 0, 0, 0))],
        out_specs=pl.BlockSpec((tq, qw), lambda p, b, i: (b * nq + i, p)),
        out_shape=jax.ShapeDtypeStruct((t, SELF_WIDTH), BF16),
        scratch_shapes=[pltpu.VMEM((2, tq + BLOCK, LANES), BF16), pltpu.VMEM((tq + BLOCK, LANES), BF16),
                        pltpu.VMEM((2, r, LANES, 2 * BLOCK), BF16),
                        pltpu.VMEM((r * B_GROUP, 2 * BLOCK, 2 * BLOCK), F32),
                        pltpu.VMEM((r * B_GROUP, 2 * BLOCK, 2 * BLOCK), BF16),
                        pltpu.VMEM((r * B_GROUP, 1, 2 * BLOCK), F32)],
        compiler_params=_params(("parallel", "parallel", "arbitrary")),
        name="swa_attention",
    )(proj, proj, proj, proj, proj, proj, proj, proj, bias, sink)


def _memattn_kernel(xq_ref, z_ref, mk_ref, mv_ref, o_ref):
    for h in range(X_HEADS):
        cols = slice(h * X_HEAD_DIM, (h + 1) * X_HEAD_DIM)
        s = _dot_nt(xq_ref[:, cols], mk_ref[:, cols])
        m = jnp.max(s, axis=1, keepdims=True)
        p = jnp.exp(s - m)
        l = jnp.sum(p, axis=1, keepdims=True)
        o = jnp.dot(p.astype(BF16), mv_ref[:, cols], preferred_element_type=F32) / l
        o_ref[:, cols] = (o * _silu(z_ref[:, cols].astype(F32))).astype(o_ref.dtype)


def mem_attention(proj, mkv, batch, seq, xq_blk, z_blk, tq=512):
    t = proj.shape[0]
    n_mem = mkv.shape[0] // batch
    tq = min(tq, seq)
    nq = seq // tq
    return pl.pallas_call(
        _memattn_kernel,
        grid=(batch, nq),
        in_specs=[pl.BlockSpec((tq, X_WIDTH), lambda b, i: (b * nq + i, xq_blk)),
                  pl.BlockSpec((tq, X_WIDTH), lambda b, i: (b * nq + i, z_blk)),
                  pl.BlockSpec((n_mem, X_WIDTH), lambda b, i: (b, 0)),
                  pl.BlockSpec((n_mem, X_WIDTH), lambda b, i: (b, 1))],
        out_specs=pl.BlockSpec((tq, X_WIDTH), lambda b, i: (b * nq + i, 0)),
        out_shape=jax.ShapeDtypeStruct((t, X_WIDTH), BF16),
        compiler_params=_params(("parallel", "arbitrary")),
        name="mem_attention",
    )(proj, proj, mkv, mkv)


def _outproj_kernel(ys_ref, ym_ref, x_ref, ws_ref, wm_ref, o_ref, ob_ref, ss_ref):
    acc = jnp.dot(ys_ref[...], ws_ref[...], preferred_element_type=F32)
    acc = acc + jnp.dot(ym_ref[...], wm_ref[...], preferred_element_type=F32)
    x = x_ref[...] + acc
    o_ref[...] = x
    ob_ref[...] = x.astype(BF16)
    part = jnp.sum(x * x, axis=-1, keepdims=True)

    @pl.when(pl.program_id(1) == 0)
    def _():
        ss_ref[...] = part

    @pl.when(pl.program_id(1) != 0)
    def _():
        ss_ref[...] += part


def out_proj(y_self, y_mem, x, w_out, layer, tm=1024, tn=512):
    t, d = x.shape
    tm = min(tm, t)
    return pl.pallas_call(
        _outproj_kernel,
        grid=(t // tm, d // tn),
        in_specs=[pl.BlockSpec((tm, SELF_WIDTH), lambda i, j: (i, 0)),
                  pl.BlockSpec((tm, X_WIDTH), lambda i, j: (i, 0)),
                  pl.BlockSpec((tm, tn), lambda i, j: (i, j)),
                  pl.BlockSpec((None, SELF_WIDTH, tn), lambda i, j: (layer, 0, j)),
                  pl.BlockSpec((None, X_WIDTH, tn), lambda i, j: (layer, SELF_WIDTH // X_WIDTH, j))],
        out_specs=[pl.BlockSpec((tm, tn), lambda i, j: (i, j)),
                   pl.BlockSpec((tm, tn), lambda i, j: (i, j)),
                   pl.BlockSpec((tm, 1), lambda i, j: (i, 0))],
        out_shape=[jax.ShapeDtypeStruct((t, d), F32), jax.ShapeDtypeStruct((t, d), BF16),
                   jax.ShapeDtypeStruct((t, 1), F32)],
        compiler_params=_params(("parallel", "arbitrary")),
        name="out_proj",
    )(y_self, y_mem, x, w_out, w_out)


A_IN_PAD = 6912


def _relayout_a_in_kernel(w_ref, g_ref, o_ref):
    o1, o2, o3 = A_Q_RANK, A_Q_RANK + A_KV_RANK, A_Q_RANK + A_KV_RANK + A_ROPE
    o4 = o3 + X_WIDTH
    x = w_ref[0] * g_ref[0]
    o_ref[0, :, 0:4096] = x[:, o4:].astype(BF16)
    o_ref[0, :, 4096:5120] = x[:, :o1].astype(BF16)
    o_ref[0, :, 5120:6144] = (x[:, o3:o4] * (X_HEAD_DIM ** -0.5)).astype(BF16)
    o_ref[0, :, 6144:6656] = x[:, o1:o2].astype(BF16)
    u = x[:, o2:o2 + LANES]
    lane = lax.broadcasted_iota(jnp.int32, u.shape, 1)
    kr = jnp.where(lane < 64, u, jnp.where(lane < 96, pltpu.roll(u, 32, 1), pltpu.roll(u, 96, 1)))
    o_ref[0, :, 6656:6784] = kr.astype(BF16)
    o_ref[0, :, 6784:] = jnp.zeros((x.shape[0], A_IN_PAD - 6784), BF16)


def relayout_a_in(w, g, tr=256):
    n_layers, k, n = w.shape
    return pl.pallas_call(
        _relayout_a_in_kernel,
        grid=(n_layers, k // tr),
        in_specs=[pl.BlockSpec((1, tr, n), lambda a, i: (a, i, 0)),
                  pl.BlockSpec((1, tr, 1), lambda a, i: (a, i, 0))],
        out_specs=pl.BlockSpec((1, tr, A_IN_PAD), lambda a, i: (a, i, 0)),
        out_shape=jax.ShapeDtypeStruct((n_layers, k, A_IN_PAD), BF16),
        compiler_params=_params(("parallel", "parallel")),
        name="relayout_a_w_in",
    )(w, g.astype(F32)[:, :, None])


def _rope_tables(positions):
    inv = ROPE_THETA ** (-jnp.arange(0, A_ROPE, 2, dtype=F32) / A_ROPE)
    ang = positions.astype(F32).reshape(-1)[:, None] * inv
    cos, sin = jnp.cos(ang), jnp.sin(ang)
    zero = jnp.zeros_like(cos)
    ct = jnp.concatenate([cos, cos, zero, zero], axis=1)
    st = jnp.concatenate([-sin, sin, zero, zero], axis=1)
    return ct, st


def _swap_halves(w):
    half = w.shape[-1] // 2
    return jnp.concatenate([w[..., half:], w[..., :half]], axis=-1)


def _prep_a_w_qb(w):
    n_layers = w.shape[0]
    w = w.astype(BF16).reshape(n_layers, A_Q_RANK, A_HEADS, A_NOPE + A_ROPE)
    rope = w[..., A_NOPE:]
    out = jnp.concatenate([w[..., :A_NOPE], rope, _swap_halves(rope)], axis=-1)
    return out.reshape(n_layers, A_Q_RANK, A_HEADS * A_QPAD)


def _prep_a_w_kvb(w):
    n_layers = w.shape[0]
    w = w.astype(BF16).reshape(n_layers, A_KV_RANK, A_HEADS, A_NOPE + A_VDIM)
    wk = w[..., :A_NOPE].reshape(n_layers, A_KV_RANK, A_HEADS * A_NOPE)
    wvt = w[..., A_NOPE:].reshape(n_layers, A_KV_RANK, A_HEADS * A_VDIM).swapaxes(1, 2)
    return wk, wvt


def _prep_b_w_in(w, g):
    nq = B_HEADS * B_HEAD_DIM
    nk = B_KV_HEADS * B_HEAD_DIM
    col = jnp.arange(w.shape[-1])
    scale = jnp.where(col < nq, B_HEAD_DIM ** -0.5,
                      jnp.where((col >= nq + 2 * nk) & (col < nq + 2 * nk + X_WIDTH), X_HEAD_DIM ** -0.5, 1.0))
    return (w * (g.astype(F32)[:, :, None] * scale.astype(F32))).astype(BF16)


def _t5_bucket(dist):
    n = jnp.maximum(dist, 0)
    nf = jnp.maximum(n, 1).astype(F32)
    large = MAX_EXACT + (jnp.log(nf / MAX_EXACT) / math.log(MAX_DIST / MAX_EXACT)
                         * (N_BUCKETS - MAX_EXACT)).astype(jnp.int32)
    large = jnp.minimum(large, N_BUCKETS - 1)
    return jnp.where(n < MAX_EXACT, n, large)


def _swa_tables(rel_bias, sinks):
    q_local = jnp.arange(BLOCK)[None, :]
    k_local = jnp.arange(2 * BLOCK)[:, None]
    dist = q_local + BLOCK - k_local
    in_window = (dist >= 0) & (dist < WINDOW)
    onehot = (_t5_bucket(dist)[:, :, None] == jnp.arange(N_BUCKETS)).astype(F32)
    bias = jnp.einsum("kqn,nh->kqh", onehot, rel_bias.astype(F32), precision=lax.Precision.HIGHEST)
    bias = jnp.where(in_window[:, :, None], bias, NEG)
    bias = bias.reshape(2 * BLOCK, BLOCK, B_PAIRS, B_GROUP, 2)
    bias = bias.transpose(2, 3, 0, 4, 1).reshape(B_PAIRS, B_GROUP, 2 * BLOCK, 2 * BLOCK)
    sink = sinks.astype(F32).reshape(B_PAIRS, B_GROUP, 2, 1)
    sink = jnp.broadcast_to(sink, (B_PAIRS, B_GROUP, 2, BLOCK)).reshape(B_PAIRS, B_GROUP, 1, 2 * BLOCK)
    return bias, sink


def kernel(x, mem, positions, norm_g, mem_norm_g, final_norm_g, w_mem_kv, w_out, a_w_in, a_q_norm_g, a_kv_norm_g,
           a_w_qb, a_w_kvb, b_w_in, b_sinks, rel_bias):
    batch, seq, d = x.shape
    n_mem = mem.shape[1]
    depth = norm_g.shape[0]
    xs = x.reshape(batch * seq, d)
    mems = mem.reshape(batch * n_mem, d)
    ct, st = _rope_tables(positions)
    w_mem_kv_b = w_mem_kv.astype(BF16)
    w_out_b = w_out.astype(BF16)
    a_w_in_b = relayout_a_in(a_w_in, norm_g[0::2])
    b_w_in_b = _prep_b_w_in(b_w_in, norm_g[1::2])
    a_w_qb_b = _prep_a_w_qb(a_w_qb)
    a_wk_b, a_wvt_b = _prep_a_w_kvb(a_w_kvb)

    xb, ss = cast_sumsq(xs)
    for i in range(depth):
        j = i // 2
        mn = rmsnorm(mems, mem_norm_g[i], BF16, "mem_rmsnorm")
        mkv = matmul(mn, w_mem_kv_b, i, BF16, "mem_kv_proj")
        if i % 2 == 0:
            proj = norm_matmul(xb, ss, a_w_in_b, j, "mla_in_proj", tn=768)
            q = mla_q(proj, a_q_norm_g[j], a_w_qb_b, j, ct, st, cq_blk=4)
            kn, vt, kr = mla_kv(proj, a_kv_norm_g[j], a_wk_b, a_wvt_b, j, ct, st, ckv_blk=12, kr_blk=52)
            y_self = mla_attention(q, proj, kn, vt, kr, batch, seq)
            y_mem = mem_attention(proj, mkv, batch, seq, xq_blk=5, z_blk=3)
        else:
            proj = norm_matmul(xb, ss, b_w_in_b, j, "swa_in_proj")
            bias, sink = _swa_tables(rel_bias, b_sinks[j])
            y_self = swa_attention(proj, bias, sink, batch, seq)
            y_mem = mem_attention(proj, mkv, batch, seq, xq_blk=4, z_blk=8)
        xs, xb, ss = out_proj(y_self, y_mem, xs, w_out_b, i)
    out = rmsnorm_from_sumsq(xs, ss, final_norm_g)
    return out.reshape(batch, seq, d)
```

```python
import functools
import math

import jax
import jax.numpy as jnp
from jax import lax
from jax.experimental import pallas as pl
from jax.experimental.pallas import tpu as pltpu

F32 = jnp.float32
BF16 = jnp.bfloat16

EPS = 1e-6
LANES = 128
NEG = -1e30
VMEM_LIMIT = 56 * 1024 * 1024

X_HEADS = 4
X_HEAD_DIM = 256
X_WIDTH = 1024
SELF_WIDTH = 3072
A_NOPE = 128
A_ROPE = 64
A_VDIM = 128
A_HEADS = 24
A_Q_RANK = 1024
A_KV_RANK = 512
SUM_ROWS = 16
A_QPAD = 256
ROPE_THETA = 10000.0
B_HEAD_DIM = 64
B_HEADS = 48
B_KV_HEADS = 8
B_GROUP = 6
B_PAIRS = B_KV_HEADS // 2
WINDOW = 128
BLOCK = 128
N_BUCKETS = 32
MAX_EXACT = 16
MAX_DIST = 128


def _params(sem):
    return pltpu.CompilerParams(dimension_semantics=sem, vmem_limit_bytes=VMEM_LIMIT)


def _rms(x, g):
    ms = jnp.mean(x * x, axis=-1, keepdims=True)
    return x * lax.rsqrt(ms + EPS) * g


def _silu(z):
    return z * jax.nn.sigmoid(z)


def _dot_nt(a, b):
    return lax.dot_general(a, b, (((1,), (1,)), ((), ())), preferred_element_type=F32)


def _rmsnorm_kernel(x_ref, g_ref, o_ref):
    o_ref[...] = _rms(x_ref[...].astype(F32), g_ref[...]).astype(o_ref.dtype)


def rmsnorm(x, g, out_dtype, name, tm=256):
    m, d = x.shape
    tm = min(tm, m)
    return pl.pallas_call(
        _rmsnorm_kernel,
        grid=(m // tm,),
        in_specs=[pl.BlockSpec((tm, d), lambda i: (i, 0)),
                  pl.BlockSpec((1, d), lambda i: (0, 0))],
        out_specs=pl.BlockSpec((tm, d), lambda i: (i, 0)),
        out_shape=jax.ShapeDtypeStruct((m, d), out_dtype),
        compiler_params=_params(("parallel",)),
        name=name,
    )(x, g.reshape(1, d).astype(F32))


def _matmul_kernel(a_ref, b_ref, o_ref):
    o_ref[...] = jnp.dot(a_ref[...], b_ref[...], preferred_element_type=F32).astype(o_ref.dtype)


def matmul(a, b, layer, out_dtype, name, tm=1024, tn=1024):
    m, k = a.shape
    n = b.shape[2]
    tm, tn = min(tm, m), min(tn, n)
    return pl.pallas_call(
        _matmul_kernel,
        grid=(m // tm, n // tn),
        in_specs=[pl.BlockSpec((tm, k), lambda i, j: (i, 0)),
                  pl.BlockSpec((None, k, tn), lambda i, j: (layer, 0, j))],
        out_specs=pl.BlockSpec((tm, tn), lambda i, j: (i, j)),
        out_shape=jax.ShapeDtypeStruct((m, n), out_dtype),
        compiler_params=_params(("parallel", "arbitrary")),
        name=name,
    )(a, b)


def _cast_sumsq_kernel(x_ref, xb_ref, ss_ref):
    x = x_ref[...]
    xb_ref[...] = x.astype(BF16)
    ss_ref[...] = jnp.sum(x * x, axis=-1, keepdims=True)


def cast_sumsq(x, tm=256):
    m, d = x.shape
    tm = min(tm, m)
    return pl.pallas_call(
        _cast_sumsq_kernel,
        grid=(m // tm,),
        in_specs=[pl.BlockSpec((tm, d), lambda i: (i, 0))],
        out_specs=[pl.BlockSpec((tm, d), lambda i: (i, 0)), pl.BlockSpec((tm, 1), lambda i: (i, 0))],
        out_shape=[jax.ShapeDtypeStruct((m, d), BF16), jax.ShapeDtypeStruct((m, 1), F32)],
        compiler_params=_params(("parallel",)),
        name="cast_sumsq",
    )(x)


def _norm_matmul_kernel(a_ref, ss_ref, b_ref, o_ref, *, width):
    r = lax.rsqrt(ss_ref[...] * (1.0 / width) + EPS)
    o_ref[...] = (jnp.dot(a_ref[...], b_ref[...], preferred_element_type=F32) * r).astype(o_ref.dtype)


def norm_matmul(a, ss, b, layer, name, tm=1024, tn=1024):
    m, k = a.shape
    n = b.shape[2]
    tm, tn = min(tm, m), min(tn, n)
    return pl.pallas_call(
        functools.partial(_norm_matmul_kernel, width=k),
        grid=(m // tm, n // tn),
        in_specs=[pl.BlockSpec((tm, k), lambda i, j: (i, 0)),
                  pl.BlockSpec((tm, 1), lambda i, j: (i, 0)),
                  pl.BlockSpec((None, k, tn), lambda i, j: (layer, 0, j))],
        out_specs=pl.BlockSpec((tm, tn), lambda i, j: (i, j)),
        out_shape=jax.ShapeDtypeStruct((m, n), BF16),
        compiler_params=_params(("parallel", "arbitrary")),
        name=name,
    )(a, ss, b)


def _scale_rows_kernel(x_ref, ss_ref, g_ref, o_ref):
    x = x_ref[...]
    r = lax.rsqrt(ss_ref[...] * (1.0 / x.shape[-1]) + EPS)
    o_ref[...] = (x * r * g_ref[...]).astype(o_ref.dtype)


def rmsnorm_from_sumsq(x, ss, g, tm=256):
    m, d = x.shape
    tm = min(tm, m)
    return pl.pallas_call(
        _scale_rows_kernel,
        grid=(m // tm,),
        in_specs=[pl.BlockSpec((tm, d), lambda i: (i, 0)),
                  pl.BlockSpec((tm, 1), lambda i: (i, 0)),
                  pl.BlockSpec((1, d), lambda i: (0, 0))],
        out_specs=pl.BlockSpec((tm, d), lambda i: (i, 0)),
        out_shape=jax.ShapeDtypeStruct((m, d), x.dtype),
        compiler_params=_params(("parallel",)),
        name="final_rmsnorm",
    )(x, ss, g.reshape(1, d).astype(F32))


def _rope(u, ct, st):
    return u * ct + pltpu.roll(u, 64, 1) * st


def _qb_kernel(cq_ref, g_ref, wt_ref, ctt_ref, stt_ref, o_ref, cqn_ref, *, heads, scale):
    @pl.when(pl.program_id(1) == 0)
    def _():
        cqn_ref[...] = _rms(cq_ref[...].astype(F32), g_ref[...]).astype(BF16)

    cqn = cqn_ref[...]
    ctt = ctt_ref[...] * scale
    stt = stt_ref[...] * scale
    for h in range(heads):
        lo = h * A_QPAD
        acc = _dot_nt(wt_ref[lo:lo + A_QPAD, :], cqn)
        o_ref[lo:lo + A_NOPE, :] = (acc[:A_NOPE] * scale).astype(BF16)
        u = acc[A_NOPE:]
        swapped = jnp.concatenate([u[2 * A_ROPE // 2:], u[:2 * A_ROPE // 2]], axis=0)
        o_ref[lo + A_NOPE:lo + A_QPAD, :] = (u * ctt + swapped * stt).astype(BF16)


def mla_q(proj, g, w_qbt, layer, ctt, stt, cq_blk, tm=1024, heads=4):
    t = proj.shape[0]
    tm = min(tm, t)
    n = A_HEADS * A_QPAD
    tn = heads * A_QPAD
    scale = (A_NOPE + A_ROPE) ** -0.5 * math.log2(math.e)
    return pl.pallas_call(
        functools.partial(_qb_kernel, heads=heads, scale=scale),
        grid=(t // tm, n // tn),
        in_specs=[pl.BlockSpec((tm, A_Q_RANK), lambda i, j: (i, cq_blk)),
                  pl.BlockSpec((1, A_Q_RANK), lambda i, j: (0, 0)),
                  pl.BlockSpec((None, tn, A_Q_RANK), lambda i, j: (layer, j, 0)),
                  pl.BlockSpec((LANES, tm), lambda i, j: (0, i)),
                  pl.BlockSpec((LANES, tm), lambda i, j: (0, i))],
        out_specs=pl.BlockSpec((tn, tm), lambda i, j: (j, i)),
        out_shape=jax.ShapeDtypeStruct((n, t), BF16),
        scratch_shapes=[pltpu.VMEM((tm, A_Q_RANK), BF16)],
        compiler_params=_params(("parallel", "arbitrary")),
        name="mla_q_up",
    )(proj, g.reshape(1, -1).astype(F32), w_qbt, ctt, stt)


def _kvb_kernel(ckv_ref, g_ref, wk_ref, wvt_ref, kr_ref, ct_ref, st_ref, kn_ref, vt_ref, kro_ref, cn_ref):
    @pl.when(pl.program_id(1) == 0)
    def _():
        cn_ref[...] = _rms(ckv_ref[...].astype(F32), g_ref[...]).astype(BF16)
        kro_ref[...] = _rope(kr_ref[...].astype(F32), ct_ref[...], st_ref[...]).astype(BF16)

    cn = cn_ref[...]
    kn_ref[...] = jnp.dot(cn, wk_ref[...], preferred_element_type=F32).astype(BF16)
    vt_ref[...] = _dot_nt(wvt_ref[...], cn).astype(BF16)


def mla_kv(proj, g, wk, wvt, layer, ct, st, ckv_blk, kr_blk, tm=1024, tn=1024):
    t = proj.shape[0]
    tm = min(tm, t)
    n = wk.shape[2]
    return pl.pallas_call(
        _kvb_kernel,
        grid=(t // tm, n // tn),
        in_specs=[pl.BlockSpec((tm, A_KV_RANK), lambda i, j: (i, ckv_blk)),
                  pl.BlockSpec((1, A_KV_RANK), lambda i, j: (0, 0)),
                  pl.BlockSpec((None, A_KV_RANK, tn), lambda i, j: (layer, 0, j)),
                  pl.BlockSpec((None, tn, A_KV_RANK), lambda i, j: (layer, j, 0)),
                  pl.BlockSpec((tm, LANES), lambda i, j: (i, kr_blk)),
                  pl.BlockSpec((tm, LANES), lambda i, j: (i, 0)),
                  pl.BlockSpec((tm, LANES), lambda i, j: (i, 0))],
        out_specs=[pl.BlockSpec((tm, tn), lambda i, j: (i, j)),
                   pl.BlockSpec((tn, tm), lambda i, j: (j, i)),
                   pl.BlockSpec((tm, LANES), lambda i, j: (i, 0))],
        out_shape=[jax.ShapeDtypeStruct((t, n), BF16),
                   jax.ShapeDtypeStruct((n, t), BF16),
                   jax.ShapeDtypeStruct((t, LANES), BF16)],
        scratch_shapes=[pltpu.VMEM((tm, A_KV_RANK), BF16)],
        compiler_params=_params(("parallel", "arbitrary")),
        name="mla_kv_up",
    )(proj, g.reshape(1, -1).astype(F32), wk, wvt, proj, ct, st)


def _mla_attn_kernel(qt_ref, z_ref, kn_ref, kr_ref, vt_ref, o_ref, sa_ref, sb_ref, m_ref, acc_ref, *,
                     tq, heads):
    qi = pl.program_id(2)
    m_ref[...] = jnp.full(m_ref.shape, NEG, F32)
    acc_ref[...] = jnp.zeros(acc_ref.shape, F32)
    ones = jnp.ones((SUM_ROWS, tq), BF16)

    def scores(j, s_ref):
        rows = pl.ds(pl.multiple_of(j * tq, tq), tq)
        kr = kr_ref[rows, :]
        for h in range(heads):
            k = jnp.concatenate([kn_ref[rows, h * A_NOPE:(h + 1) * A_NOPE], kr], axis=1)
            s_ref[h] = jnp.dot(k, qt_ref[h * A_QPAD:(h + 1) * A_QPAD, :], preferred_element_type=F32)

    def update(j, s_ref, masked):
        rows = pl.ds(pl.multiple_of(j * tq, tq), tq)
        for h in range(heads):
            s = s_ref[h]
            if masked:
                r = lax.broadcasted_iota(jnp.int32, s.shape, 0)
                c = lax.broadcasted_iota(jnp.int32, s.shape, 1)
                s = jnp.where(r <= c, s, NEG)
            m_prev = m_ref[h]
            m_cur = jnp.maximum(m_prev, jnp.max(s, axis=0, keepdims=True))
            alpha = jnp.exp2(m_prev - m_cur)
            p = jnp.exp2(s - m_cur).astype(BF16)
            va = jnp.concatenate([vt_ref[h * A_VDIM:(h + 1) * A_VDIM, rows], ones], axis=0)
            acc_ref[h] = alpha * acc_ref[h] + jnp.dot(va, p, preferred_element_type=F32)
            m_ref[h] = m_cur

    even = (qi % 2) == 1

    @pl.when(even)
    def _():
        scores(0, sb_ref)
        scores(1, sa_ref)
        update(0, sb_ref, False)

    @pl.when(jnp.logical_not(even))
    def _():
        scores(0, sa_ref)

    first = even.astype(jnp.int32)

    def pair(t, carry):
        c = first + 2 * t
        scores(c + 1, sb_ref)
        update(c, sa_ref, False)
        scores(c + 2, sa_ref)
        update(c + 1, sb_ref, False)
        return carry

    lax.fori_loop(0, (qi - first) // 2, pair, 0)
    update(qi, sa_ref, True)
    for h in range(heads):
        acc = acc_ref[h]
        o = (acc[:A_VDIM] * (1.0 / acc[A_VDIM:A_VDIM + 1])).T
        gate = _silu(z_ref[:, h * A_VDIM:(h + 1) * A_VDIM].astype(F32))
        o_ref[:, h * A_VDIM:(h + 1) * A_VDIM] = (o * gate).astype(o_ref.dtype)


def mla_attention(qt, proj, kn, vt, kr, batch, seq, tq=512, heads=6):
    t = proj.shape[0]
    tq = min(tq, seq)
    nq = seq // tq
    return pl.pallas_call(
        functools.partial(_mla_attn_kernel, tq=tq, heads=heads),
        grid=(batch, A_HEADS // heads, nq),
        in_specs=[pl.BlockSpec((heads * A_QPAD, tq), lambda b, h, i: (h, b * nq + i)),
                  pl.BlockSpec((tq, heads * A_VDIM), lambda b, h, i: (b * nq + i, h)),
                  pl.BlockSpec((seq, heads * A_NOPE), lambda b, h, i: (b, h)),
                  pl.BlockSpec((seq, LANES), lambda b, h, i: (b, 0)),
                  pl.BlockSpec((heads * A_VDIM, seq), lambda b, h, i: (h, b))],
        out_specs=pl.BlockSpec((tq, heads * A_VDIM), lambda b, h, i: (b * nq + i, h)),
        out_shape=jax.ShapeDtypeStruct((t, A_HEADS * A_VDIM), BF16),
        scratch_shapes=[pltpu.VMEM((heads, tq, tq), F32), pltpu.VMEM((heads, tq, tq), F32),
                        pltpu.VMEM((heads, 1, tq), F32),
                        pltpu.VMEM((heads, A_VDIM + SUM_ROWS, tq), F32)],
        compiler_params=_params(("parallel", "parallel", "arbitrary")),
        name="mla_attention",
    )(qt, proj, kn, kr, vt)


def _swa_kernel(q_ref, z0_ref, z1_ref, z2_ref, kc_ref, kp_ref, vc_ref, vp_ref, bias_ref, sink_ref, o_ref,
                kd_ref, vf_ref, vt_ref, s_ref, p_ref, inv_ref, *, tq):
    first = pl.program_id(2) == 0
    r = tq // BLOCK
    low = lax.broadcasted_iota(jnp.int32, (BLOCK, LANES), 1) < B_HEAD_DIM
    for lo, src in ((0, kp_ref), (BLOCK, kc_ref)):
        k = src[...].astype(F32)
        kr = pltpu.roll(k, B_HEAD_DIM, 1)
        half = lax.broadcasted_iota(jnp.int32, k.shape, 1) < B_HEAD_DIM
        kd_ref[0, lo:lo + k.shape[0], :] = jnp.where(half, k, kr).astype(BF16)
        kd_ref[1, lo:lo + k.shape[0], :] = jnp.where(half, kr, k).astype(BF16)
    vf_ref[0:BLOCK, :] = vp_ref[...]
    vf_ref[BLOCK:, :] = vc_ref[...]
    for c in range(r):
        vt = vf_ref[c * BLOCK:(c + 2) * BLOCK, :].astype(F32).T
        for e in range(2):
            ve = vt[e * B_HEAD_DIM:(e + 1) * B_HEAD_DIM]
            vt_ref[e, c] = jnp.concatenate([ve, ve], axis=0).astype(BF16)
    key = lax.broadcasted_iota(jnp.int32, (2 * BLOCK, 2 * BLOCK), 0)
    pen0 = jnp.where((key < BLOCK) & first, NEG, 0.0).astype(F32)
    tiles = [(c, l) for c in range(r) for l in range(B_GROUP)]
    for t, (c, l) in enumerate(tiles):
        qp = q_ref[c * BLOCK:(c + 1) * BLOCK, l * LANES:(l + 1) * LANES]
        zero = jnp.zeros_like(qp)
        q2 = jnp.concatenate([jnp.where(low, qp, zero), jnp.where(low, zero, qp)], axis=0)
        s = _dot_nt(kd_ref[l // 3, c * BLOCK:(c + 2) * BLOCK, :], q2) + bias_ref[0, l]
        s_ref[t] = s + pen0 if c == 0 else s
    for t, (c, l) in enumerate(tiles):
        s = s_ref[t]
        sink = sink_ref[0, l]
        m = jnp.maximum(jnp.max(s, axis=0, keepdims=True), sink)
        p = jnp.exp(s - m)
        inv_ref[t] = 1.0 / (jnp.sum(p, axis=0, keepdims=True) + jnp.exp(sink - m))
        p_ref[t] = p.astype(BF16)
    z_refs = (z0_ref, z1_ref, z2_ref)
    for t, (c, l) in enumerate(tiles):
        rows = slice(c * BLOCK, (c + 1) * BLOCK)
        ot = jnp.dot(vt_ref[l // 3, c], p_ref[t], preferred_element_type=F32) * inv_ref[t]
        o = jnp.concatenate([ot[0:B_HEAD_DIM, 0:BLOCK], ot[B_HEAD_DIM:, BLOCK:]], axis=0).T
        z = z_refs[l // 2][rows, (l % 2) * LANES:(l % 2 + 1) * LANES].astype(F32)
        o_ref[rows, l * LANES:(l + 1) * LANES] = (o * _silu(z)).astype(o_ref.dtype)


def swa_attention(proj, bias, sink, batch, seq, tq=512):
    t = proj.shape[0]
    tq = min(tq, seq)
    nq = seq // tq
    r = tq // BLOCK
    qw = B_GROUP * LANES
    k_off, v_off = 3072 // LANES, 3584 // LANES
    z_off = 5120 // (2 * LANES)

    def prev_map(p, b, i, off):
        return (jnp.maximum(b * (seq // BLOCK) + i * r - 1, b * (seq // BLOCK)), off + p)

    def z_map(p, b, i, part):
        return (b * nq + i, z_off + 3 * p + part)

    return pl.pallas_call(
        functools.partial(_swa_kernel, tq=tq),
        grid=(B_PAIRS, batch, nq),
        in_specs=[pl.BlockSpec((tq, qw), lambda p, b, i: (b * nq + i, p)),
                  pl.BlockSpec((tq, 2 * LANES), functools.partial(z_map, part=0)),
                  pl.BlockSpec((tq, 2 * LANES), functools.partial(z_map, part=1)),
                  pl.BlockSpec((tq, 2 * LANES), functools.partial(z_map, part=2)),
                  pl.BlockSpec((tq, LANES), lambda p, b, i: (b * nq + i, k_off + p)),
                  pl.BlockSpec((BLOCK, LANES), functools.partial(prev_map, off=k_off)),
                  pl.BlockSpec((tq, LANES), lambda p, b, i: (b * nq + i, v_off + p)),
                  pl.BlockSpec((BLOCK, LANES), functools.partial(prev_map, off=v_off)),
                  pl.BlockSpec((1, B_GROUP, 2 * BLOCK, 2 * BLOCK), lambda p, b, i: (p, 0, 0, 0)),
                  pl.BlockSpec((1, B_GROUP, 1, 2 * BLOCK), lambda p, b, i: (p, 0, 0, 0))],
        out_specs=pl.BlockSpec((tq, qw), lambda p, b, i: (b * nq + i, p)),
        out_shape=jax.ShapeDtypeStruct((t, SELF_WIDTH), BF16),
        scratch_shapes=[pltpu.VMEM((2, tq + BLOCK, LANES), BF16), pltpu.VMEM((tq + BLOCK, LANES), BF16),
                        pltpu.VMEM((2, r, LANES, 2 * BLOCK), BF16),
                        pltpu.VMEM((r * B_GROUP, 2 * BLOCK, 2 * BLOCK), F32),
                        pltpu.VMEM((r * B_GROUP, 2 * BLOCK, 2 * BLOCK), BF16),
                        pltpu.VMEM((r * B_GROUP, 1, 2 * BLOCK), F32)],
        compiler_params=_params(("parallel", "parallel", "arbitrary")),
        name="swa_attention",
    )(proj, proj, proj, proj, proj, proj, proj, proj, bias, sink)


def _memattn_kernel(xq_ref, z_ref, mk_ref, mv_ref, o_ref):
    for h in range(X_HEADS):
        cols = slice(h * X_HEAD_DIM, (h + 1) * X_HEAD_DIM)
        s = _dot_nt(xq_ref[:, cols], mk_ref[:, cols])
        m = jnp.max(s, axis=1, keepdims=True)
        p = jnp.exp(s - m)
        l = jnp.sum(p, axis=1, keepdims=True)
        o = jnp.dot(p.astype(BF16), mv_ref[:, cols], preferred_element_type=F32) / l
        o_ref[:, cols] = (o * _silu(z_ref[:, cols].astype(F32))).astype(o_ref.dtype)


def mem_attention(proj, mkv, batch, seq, xq_blk, z_blk, tq=512):
    t = proj.shape[0]
    n_mem = mkv.shape[0] // batch
    tq = min(tq, seq)
    nq = seq // tq
    return pl.pallas_call(
        _memattn_kernel,
        grid=(batch, nq),
        in_specs=[pl.BlockSpec((tq, X_WIDTH), lambda b, i: (b * nq + i, xq_blk)),
                  pl.BlockSpec((tq, X_WIDTH), lambda b, i: (b * nq + i, z_blk)),
                  pl.BlockSpec((n_mem, X_WIDTH), lambda b, i: (b, 0)),
                  pl.BlockSpec((n_mem, X_WIDTH), lambda b, i: (b, 1))],
        out_specs=pl.BlockSpec((tq, X_WIDTH), lambda b, i: (b * nq + i, 0)),
        out_shape=jax.ShapeDtypeStruct((t, X_WIDTH), BF16),
        compiler_params=_params(("parallel", "arbitrary")),
        name="mem_attention",
    )(proj, proj, mkv, mkv)


def _outproj_kernel(ys_ref, ym_ref, x_ref, ws_ref, wm_ref, o_ref, ob_ref, ss_ref):
    acc = jnp.dot(ys_ref[...], ws_ref[...], preferred_element_type=F32)
    acc = acc + jnp.dot(ym_ref[...], wm_ref[...], preferred_element_type=F32)
    x = x_ref[...] + acc
    o_ref[...] = x
    ob_ref[...] = x.astype(BF16)
    part = jnp.sum(x * x, axis=-1, keepdims=True)

    @pl.when(pl.program_id(1) == 0)
    def _():
        ss_ref[...] = part

    @pl.when(pl.program_id(1) != 0)
    def _():
        ss_ref[...] += part


def out_proj(y_self, y_mem, x, w_out, layer, tm=1024, tn=512):
    t, d = x.shape
    tm = min(tm, t)
    return pl.pallas_call(
        _outproj_kernel,
        grid=(t // tm, d // tn),
        in_specs=[pl.BlockSpec((tm, SELF_WIDTH), lambda i, j: (i, 0)),
                  pl.BlockSpec((tm, X_WIDTH), lambda i, j: (i, 0)),
                  pl.BlockSpec((tm, tn), lambda i, j: (i, j)),
                  pl.BlockSpec((None, SELF_WIDTH, tn), lambda i, j: (layer, 0, j)),
                  pl.BlockSpec((None, X_WIDTH, tn), lambda i, j: (layer, SELF_WIDTH // X_WIDTH, j))],
        out_specs=[pl.BlockSpec((tm, tn), lambda i, j: (i, j)),
                   pl.BlockSpec((tm, tn), lambda i, j: (i, j)),
                   pl.BlockSpec((tm, 1), lambda i, j: (i, 0))],
        out_shape=[jax.ShapeDtypeStruct((t, d), F32), jax.ShapeDtypeStruct((t, d), BF16),
                   jax.ShapeDtypeStruct((t, 1), F32)],
        compiler_params=_params(("parallel", "arbitrary")),
        name="out_proj",
    )(y_self, y_mem, x, w_out, w_out)


A_IN_PAD = 6912


def _relayout_a_in_kernel(w_ref, g_ref, o_ref):
    o1, o2, o3 = A_Q_RANK, A_Q_RANK + A_KV_RANK, A_Q_RANK + A_KV_RANK + A_ROPE
    o4 = o3 + X_WIDTH
    x = w_ref[0] * g_ref[0]
    o_ref[0, :, 0:4096] = x[:, o4:].astype(BF16)
    o_ref[0, :, 4096:5120] = x[:, :o1].astype(BF16)
    o_ref[0, :, 5120:6144] = (x[:, o3:o4] * (X_HEAD_DIM ** -0.5)).astype(BF16)
    o_ref[0, :, 6144:6656] = x[:, o1:o2].astype(BF16)
    u = x[:, o2:o2 + LANES]
    lane = lax.broadcasted_iota(jnp.int32, u.shape, 1)
    kr = jnp.where(lane < 64, u, jnp.where(lane < 96, pltpu.roll(u, 32, 1), pltpu.roll(u, 96, 1)))
    o_ref[0, :, 6656:6784] = kr.astype(BF16)
    o_ref[0, :, 6784:] = jnp.zeros((x.shape[0], A_IN_PAD - 6784), BF16)


def relayout_a_in(w, g, tr=256):
    n_layers, k, n = w.shape
    return pl.pallas_call(
        _relayout_a_in_kernel,
        grid=(n_layers, k // tr),
        in_specs=[pl.BlockSpec((1, tr, n), lambda a, i: (a, i, 0)),
                  pl.BlockSpec((1, tr, 1), lambda a, i: (a, i, 0))],
        out_specs=pl.BlockSpec((1, tr, A_IN_PAD), lambda a, i: (a, i, 0)),
        out_shape=jax.ShapeDtypeStruct((n_layers, k, A_IN_PAD), BF16),
        compiler_params=_params(("parallel", "parallel")),
        name="relayout_a_w_in",
    )(w, g.astype(F32)[:, :, None])


def _rope_tables(positions):
    inv = ROPE_THETA ** (-jnp.arange(0, A_ROPE, 2, dtype=F32) / A_ROPE)
    ang = positions.astype(F32).reshape(-1)[:, None] * inv
    cos, sin = jnp.cos(ang), jnp.sin(ang)
    zero = jnp.zeros_like(cos)
    ct = jnp.concatenate([cos, cos, zero, zero], axis=1)
    st = jnp.concatenate([-sin, sin, zero, zero], axis=1)
    return ct, st


def _swap_halves(w):
    half = w.shape[-1] // 2
    return jnp.concatenate([w[..., half:], w[..., :half]], axis=-1)


def _prep_a_w_qb(w):
    n_layers = w.shape[0]
    w = w.astype(BF16).reshape(n_layers, A_Q_RANK, A_HEADS, A_NOPE + A_ROPE)
    rope = w[..., A_NOPE:]
    out = jnp.concatenate([w[..., :A_NOPE], rope, _swap_halves(rope)], axis=-1)
    return out.reshape(n_layers, A_Q_RANK, A_HEADS * A_QPAD).swapaxes(1, 2)


def _prep_a_w_kvb(w):
    n_layers = w.shape[0]
    w = w.astype(BF16).reshape(n_layers, A_KV_RANK, A_HEADS, A_NOPE + A_VDIM)
    wk = w[..., :A_NOPE].reshape(n_layers, A_KV_RANK, A_HEADS * A_NOPE)
    wvt = w[..., A_NOPE:].reshape(n_layers, A_KV_RANK, A_HEADS * A_VDIM).swapaxes(1, 2)
    return wk, wvt


def _prep_b_w_in(w, g):
    nq = B_HEADS * B_HEAD_DIM
    nk = B_KV_HEADS * B_HEAD_DIM
    col = jnp.arange(w.shape[-1])
    scale = jnp.where(col < nq, B_HEAD_DIM ** -0.5,
                      jnp.where((col >= nq + 2 * nk) & (col < nq + 2 * nk + X_WIDTH), X_HEAD_DIM ** -0.5, 1.0))
    return (w * (g.astype(F32)[:, :, None] * scale.astype(F32))).astype(BF16)


def _t5_bucket(dist):
    n = jnp.maximum(dist, 0)
    nf = jnp.maximum(n, 1).astype(F32)
    large = MAX_EXACT + (jnp.log(nf / MAX_EXACT) / math.log(MAX_DIST / MAX_EXACT)
                         * (N_BUCKETS - MAX_EXACT)).astype(jnp.int32)
    large = jnp.minimum(large, N_BUCKETS - 1)
    return jnp.where(n < MAX_EXACT, n, large)


def _swa_tables(rel_bias, sinks):
    q_local = jnp.arange(BLOCK)[None, :]
    k_local = jnp.arange(2 * BLOCK)[:, None]
    dist = q_local + BLOCK - k_local
    in_window = (dist >= 0) & (dist < WINDOW)
    onehot = (_t5_bucket(dist)[:, :, None] == jnp.arange(N_BUCKETS)).astype(F32)
    bias = jnp.einsum("kqn,nh->kqh", onehot, rel_bias.astype(F32), precision=lax.Precision.HIGHEST)
    bias = jnp.where(in_window[:, :, None], bias, NEG)
    bias = bias.reshape(2 * BLOCK, BLOCK, B_PAIRS, B_GROUP, 2)
    bias = bias.transpose(2, 3, 0, 4, 1).reshape(B_PAIRS, B_GROUP, 2 * BLOCK, 2 * BLOCK)
    sink = sinks.astype(F32).reshape(B_PAIRS, B_GROUP, 2, 1)
    sink = jnp.broadcast_to(sink, (B_PAIRS, B_GROUP, 2, BLOCK)).reshape(B_PAIRS, B_GROUP, 1, 2 * BLOCK)
    return bias, sink


def kernel(x, mem, positions, norm_g, mem_norm_g, final_norm_g, w_mem_kv, w_out, a_w_in, a_q_norm_g, a_kv_norm_g,
           a_w_qb, a_w_kvb, b_w_in, b_sinks, rel_bias):
    batch, seq, d = x.shape
    n_mem = mem.shape[1]
    depth = norm_g.shape[0]
    xs = x.reshape(batch * seq, d)
    mems = mem.reshape(batch * n_mem, d)
    ct, st = _rope_tables(positions)
    ctt, stt = ct.T, st.T
    w_mem_kv_b = w_mem_kv.astype(BF16)
    w_out_b = w_out.astype(BF16)
    a_w_in_b = relayout_a_in(a_w_in, norm_g[0::2])
    b_w_in_b = _prep_b_w_in(b_w_in, norm_g[1::2])
    a_w_qb_b = _prep_a_w_qb(a_w_qb)
    a_wk_b, a_wvt_b = _prep_a_w_kvb(a_w_kvb)

    xb, ss = cast_sumsq(xs)
    for i in range(depth):
        j = i // 2
        mn = rmsnorm(mems, mem_norm_g[i], BF16, "mem_rmsnorm")
        mkv = matmul(mn, w_mem_kv_b, i, BF16, "mem_kv_proj")
        if i % 2 == 0:
            proj = norm_matmul(xb, ss, a_w_in_b, j, "mla_in_proj", tn=768)
            qt = mla_q(proj, a_q_norm_g[j], a_w_qb_b, j, ctt, stt, cq_blk=4)
            kn, vt, kr = mla_kv(proj, a_kv_norm_g[j], a_wk_b, a_wvt_b, j, ct, st, ckv_blk=12, kr_blk=52)
            y_self = mla_attention(qt, proj, kn, vt, kr, batch, seq)
            y_mem = mem_attention(proj, mkv, batch, seq, xq_blk=5, z_blk=3)
        else:
            proj = norm_matmul(xb, ss, b_w_in_b, j, "swa_in_proj")
            bias, sink = _swa_tables(rel_bias, b_sinks[j])
            y_self = swa_attention(proj, bias, sink, batch, seq)
            y_mem = mem_attention(proj, mkv, batch, seq, xq_blk=4, z_blk=8)
        xs, xb, ss = out_proj(y_self, y_mem, xs, w_out_b, i)
    out = rmsnorm_from_sumsq(xs, ss, final_norm_g)
    return out.reshape(batch, seq, d)
```

```python
import functools
import math

import jax
import jax.numpy as jnp
from jax import lax
from jax.experimental import pallas as pl
from jax.experimental.pallas import tpu as pltpu

F32 = jnp.float32
BF16 = jnp.bfloat16

EPS = 1e-6
LANES = 128
NEG = -1e30
VMEM_LIMIT = 56 * 1024 * 1024

X_HEADS = 4
X_HEAD_DIM = 256
X_WIDTH = 1024
SELF_WIDTH = 3072
A_NOPE = 128
A_ROPE = 64
A_VDIM = 128
A_HEADS = 24
A_Q_RANK = 1024
A_KV_RANK = 512
SUM_ROWS = 16
A_QPAD = 256
ROPE_THETA = 10000.0
B_HEAD_DIM = 64
B_HEADS = 48
B_KV_HEADS = 8
B_GROUP = 6
B_PAIRS = B_KV_HEADS // 2
WINDOW = 128
BLOCK = 128
N_BUCKETS = 32
MAX_EXACT = 16
MAX_DIST = 128


def _params(sem):
    return pltpu.CompilerParams(dimension_semantics=sem, vmem_limit_bytes=VMEM_LIMIT)


def _rms(x, g):
    ms = jnp.mean(x * x, axis=-1, keepdims=True)
    return x * lax.rsqrt(ms + EPS) * g


def _silu(z):
    return z * jax.nn.sigmoid(z)


def _dot_nt(a, b):
    return lax.dot_general(a, b, (((1,), (1,)), ((), ())), preferred_element_type=F32)


def _mem_kv_kernel(mem_ref, g_ref, w_ref, o_ref, mn_ref):
    @pl.when(pl.program_id(2) == 0)
    def _():
        mn_ref[...] = _rms(mem_ref[...], g_ref[...]).astype(BF16)

    o_ref[...] = jnp.dot(mn_ref[...], w_ref[...], preferred_element_type=F32).astype(o_ref.dtype)


def mem_kv_all(mem, g, w, tm=512, tn=1024):
    m, d = mem.shape
    n_layers, _, n = w.shape
    tm = min(tm, m)
    return pl.pallas_call(
        _mem_kv_kernel,
        grid=(n_layers, m // tm, n // tn),
        in_specs=[pl.BlockSpec((tm, d), lambda l, i, j: (i, 0)),
                  pl.BlockSpec((None, 1, d), lambda l, i, j: (l, 0, 0)),
                  pl.BlockSpec((None, d, tn), lambda l, i, j: (l, 0, j))],
        out_specs=pl.BlockSpec((None, tm, tn), lambda l, i, j: (l, i, j)),
        out_shape=jax.ShapeDtypeStruct((n_layers, m, n), BF16),
        scratch_shapes=[pltpu.VMEM((tm, d), BF16)],
        compiler_params=_params(("parallel", "parallel", "arbitrary")),
        name="mem_kv_proj",
    )(mem, g.astype(F32)[:, None, :], w)


def _cast_sumsq_kernel(x_ref, xb_ref, ss_ref):
    x = x_ref[...]
    xb_ref[...] = x.astype(BF16)
    ss_ref[...] = jnp.sum(x * x, axis=-1, keepdims=True)


def cast_sumsq(x, tm=256):
    m, d = x.shape
    tm = min(tm, m)
    return pl.pallas_call(
        _cast_sumsq_kernel,
        grid=(m // tm,),
        in_specs=[pl.BlockSpec((tm, d), lambda i: (i, 0))],
        out_specs=[pl.BlockSpec((tm, d), lambda i: (i, 0)), pl.BlockSpec((tm, 1), lambda i: (i, 0))],
        out_shape=[jax.ShapeDtypeStruct((m, d), BF16), jax.ShapeDtypeStruct((m, 1), F32)],
        compiler_params=_params(("parallel",)),
        name="cast_sumsq",
    )(x)


def _norm_matmul_kernel(a_ref, ss_ref, b_ref, o_ref, *, width):
    r = lax.rsqrt(ss_ref[...] * (1.0 / width) + EPS)
    o_ref[...] = (jnp.dot(a_ref[...], b_ref[...], preferred_element_type=F32) * r).astype(o_ref.dtype)


def norm_matmul(a, ss, b, layer, name, tm=1024, tn=1024):
    m, k = a.shape
    n = b.shape[2]
    tm, tn = min(tm, m), min(tn, n)
    return pl.pallas_call(
        functools.partial(_norm_matmul_kernel, width=k),
        grid=(m // tm, n // tn),
        in_specs=[pl.BlockSpec((tm, k), lambda i, j: (i, 0)),
                  pl.BlockSpec((tm, 1), lambda i, j: (i, 0)),
                  pl.BlockSpec((None, k, tn), lambda i, j: (layer, 0, j))],
        out_specs=pl.BlockSpec((tm, tn), lambda i, j: (i, j)),
        out_shape=jax.ShapeDtypeStruct((m, n), BF16),
        compiler_params=_params(("parallel", "arbitrary")),
        name=name,
    )(a, ss, b)


def _scale_rows_kernel(x_ref, ss_ref, g_ref, o_ref):
    x = x_ref[...]
    r = lax.rsqrt(ss_ref[...] * (1.0 / x.shape[-1]) + EPS)
    o_ref[...] = (x * r * g_ref[...]).astype(o_ref.dtype)


def rmsnorm_from_sumsq(x, ss, g, tm=256):
    m, d = x.shape
    tm = min(tm, m)
    return pl.pallas_call(
        _scale_rows_kernel,
        grid=(m // tm,),
        in_specs=[pl.BlockSpec((tm, d), lambda i: (i, 0)),
                  pl.BlockSpec((tm, 1), lambda i: (i, 0)),
                  pl.BlockSpec((1, d), lambda i: (0, 0))],
        out_specs=pl.BlockSpec((tm, d), lambda i: (i, 0)),
        out_shape=jax.ShapeDtypeStruct((m, d), x.dtype),
        compiler_params=_params(("parallel",)),
        name="final_rmsnorm",
    )(x, ss, g.reshape(1, d).astype(F32))


def _rope(u, ct, st):
    return u * ct + pltpu.roll(u, 64, 1) * st


def _qb_kernel(cq_ref, g_ref, wt_ref, ctt_ref, stt_ref, o_ref, cqn_ref, *, heads, scale):
    @pl.when(pl.program_id(1) == 0)
    def _():
        cqn_ref[...] = _rms(cq_ref[...].astype(F32), g_ref[...]).astype(BF16)

    cqn = cqn_ref[...]
    ctt = ctt_ref[...] * scale
    stt = stt_ref[...] * scale
    for h in range(heads):
        lo = h * A_QPAD
        acc = _dot_nt(wt_ref[lo:lo + A_QPAD, :], cqn)
        o_ref[lo:lo + A_NOPE, :] = (acc[:A_NOPE] * scale).astype(BF16)
        u = acc[A_NOPE:]
        swapped = jnp.concatenate([u[2 * A_ROPE // 2:], u[:2 * A_ROPE // 2]], axis=0)
        o_ref[lo + A_NOPE:lo + A_QPAD, :] = (u * ctt + swapped * stt).astype(BF16)


def mla_q(proj, g, w_qbt, layer, ctt, stt, cq_blk, tm=1024, heads=4):
    t = proj.shape[0]
    tm = min(tm, t)
    n = A_HEADS * A_QPAD
    tn = heads * A_QPAD
    scale = (A_NOPE + A_ROPE) ** -0.5 * math.log2(math.e)
    return pl.pallas_call(
        functools.partial(_qb_kernel, heads=heads, scale=scale),
        grid=(t // tm, n // tn),
        in_specs=[pl.BlockSpec((tm, A_Q_RANK), lambda i, j: (i, cq_blk)),
                  pl.BlockSpec((1, A_Q_RANK), lambda i, j: (0, 0)),
                  pl.BlockSpec((None, tn, A_Q_RANK), lambda i, j: (layer, j, 0)),
                  pl.BlockSpec((LANES, tm), lambda i, j: (0, i)),
                  pl.BlockSpec((LANES, tm), lambda i, j: (0, i))],
        out_specs=pl.BlockSpec((tn, tm), lambda i, j: (j, i)),
        out_shape=jax.ShapeDtypeStruct((n, t), BF16),
        scratch_shapes=[pltpu.VMEM((tm, A_Q_RANK), BF16)],
        compiler_params=_params(("parallel", "arbitrary")),
        name="mla_q_up",
    )(proj, g.reshape(1, -1).astype(F32), w_qbt, ctt, stt)


def _kvb_kernel(ckv_ref, g_ref, wk_ref, wvt_ref, kr_ref, ct_ref, st_ref, kn_ref, vt_ref, kro_ref, cn_ref):
    @pl.when(pl.program_id(1) == 0)
    def _():
        cn_ref[...] = _rms(ckv_ref[...].astype(F32), g_ref[...]).astype(BF16)
        kro_ref[...] = _rope(kr_ref[...].astype(F32), ct_ref[...], st_ref[...]).astype(BF16)

    cn = cn_ref[...]
    kn_ref[...] = jnp.dot(cn, wk_ref[...], preferred_element_type=F32).astype(BF16)
    vt_ref[...] = _dot_nt(wvt_ref[...], cn).astype(BF16)


def mla_kv(proj, g, wk, wvt, layer, ct, st, ckv_blk, kr_blk, tm=1024, tn=1024):
    t = proj.shape[0]
    tm = min(tm, t)
    n = wk.shape[2]
    return pl.pallas_call(
        _kvb_kernel,
        grid=(t // tm, n // tn),
        in_specs=[pl.BlockSpec((tm, A_KV_RANK), lambda i, j: (i, ckv_blk)),
                  pl.BlockSpec((1, A_KV_RANK), lambda i, j: (0, 0)),
                  pl.BlockSpec((None, A_KV_RANK, tn), lambda i, j: (layer, 0, j)),
                  pl.BlockSpec((None, tn, A_KV_RANK), lambda i, j: (layer, j, 0)),
                  pl.BlockSpec((tm, LANES), lambda i, j: (i, kr_blk)),
                  pl.BlockSpec((tm, LANES), lambda i, j: (i, 0)),
                  pl.BlockSpec((tm, LANES), lambda i, j: (i, 0))],
        out_specs=[pl.BlockSpec((tm, tn), lambda i, j: (i, j)),
                   pl.BlockSpec((tn, tm), lambda i, j: (j, i)),
                   pl.BlockSpec((tm, LANES), lambda i, j: (i, 0))],
        out_shape=[jax.ShapeDtypeStruct((t, n), BF16),
                   jax.ShapeDtypeStruct((n, t), BF16),
                   jax.ShapeDtypeStruct((t, LANES), BF16)],
        scratch_shapes=[pltpu.VMEM((tm, A_KV_RANK), BF16)],
        compiler_params=_params(("parallel", "arbitrary")),
        name="mla_kv_up",
    )(proj, g.reshape(1, -1).astype(F32), wk, wvt, proj, ct, st)


def _mla_attn_kernel(qt_ref, z_ref, kn_ref, kr_ref, vt_ref, o_ref, sa_ref, sb_ref, m_ref, acc_ref, *,
                     tq, heads):
    qi = pl.program_id(2)
    m_ref[...] = jnp.full(m_ref.shape, NEG, F32)
    acc_ref[...] = jnp.zeros(acc_ref.shape, F32)
    ones = jnp.ones((SUM_ROWS, tq), BF16)

    def scores(j, s_ref):
        rows = pl.ds(pl.multiple_of(j * tq, tq), tq)
        kr = kr_ref[rows, :]
        for h in range(heads):
            k = jnp.concatenate([kn_ref[rows, h * A_NOPE:(h + 1) * A_NOPE], kr], axis=1)
            s_ref[h] = jnp.dot(k, qt_ref[h * A_QPAD:(h + 1) * A_QPAD, :], preferred_element_type=F32)

    def update(j, s_ref, masked):
        rows = pl.ds(pl.multiple_of(j * tq, tq), tq)
        for h in range(heads):
            s = s_ref[h]
            if masked:
                r = lax.broadcasted_iota(jnp.int32, s.shape, 0)
                c = lax.broadcasted_iota(jnp.int32, s.shape, 1)
                s = jnp.where(r <= c, s, NEG)
            m_prev = m_ref[h]
            m_cur = jnp.maximum(m_prev, jnp.max(s, axis=0, keepdims=True))
            alpha = jnp.exp2(m_prev - m_cur)
            p = jnp.exp2(s - m_cur).astype(BF16)
            va = jnp.concatenate([vt_ref[h * A_VDIM:(h + 1) * A_VDIM, rows], ones], axis=0)
            acc_ref[h] = alpha * acc_ref[h] + jnp.dot(va, p, preferred_element_type=F32)
            m_ref[h] = m_cur

    even = (qi % 2) == 1

    @pl.when(even)
    def _():
        scores(0, sb_ref)
        scores(1, sa_ref)
        update(0, sb_ref, False)

    @pl.when(jnp.logical_not(even))
    def _():
        scores(0, sa_ref)

    first = even.astype(jnp.int32)

    def pair(t, carry):
        c = first + 2 * t
        scores(c + 1, sb_ref)
        update(c, sa_ref, False)
        scores(c + 2, sa_ref)
        update(c + 1, sb_ref, False)
        return carry

    lax.fori_loop(0, (qi - first) // 2, pair, 0)
    update(qi, sa_ref, True)
    for h in range(heads):
        acc = acc_ref[h]
        o = (acc[:A_VDIM] * (1.0 / acc[A_VDIM:A_VDIM + 1])).T
        gate = _silu(z_ref[:, h * A_VDIM:(h + 1) * A_VDIM].astype(F32))
        o_ref[:, h * A_VDIM:(h + 1) * A_VDIM] = (o * gate).astype(o_ref.dtype)


def mla_attention(qt, proj, kn, vt, kr, batch, seq, tq=512, heads=6):
    t = proj.shape[0]
    tq = min(tq, seq)
    nq = seq // tq
    return pl.pallas_call(
        functools.partial(_mla_attn_kernel, tq=tq, heads=heads),
        grid=(batch, A_HEADS // heads, nq),
        in_specs=[pl.BlockSpec((heads * A_QPAD, tq), lambda b, h, i: (h, b * nq + i)),
                  pl.BlockSpec((tq, heads * A_VDIM), lambda b, h, i: (b * nq + i, h)),
                  pl.BlockSpec((seq, heads * A_NOPE), lambda b, h, i: (b, h)),
                  pl.BlockSpec((seq, LANES), lambda b, h, i: (b, 0)),
                  pl.BlockSpec((heads * A_VDIM, seq), lambda b, h, i: (h, b))],
        out_specs=pl.BlockSpec((tq, heads * A_VDIM), lambda b, h, i: (b * nq + i, h)),
        out_shape=jax.ShapeDtypeStruct((t, A_HEADS * A_VDIM), BF16),
        scratch_shapes=[pltpu.VMEM((heads, tq, tq), F32), pltpu.VMEM((heads, tq, tq), F32),
                        pltpu.VMEM((heads, 1, tq), F32),
                        pltpu.VMEM((heads, A_VDIM + SUM_ROWS, tq), F32)],
        compiler_params=_params(("parallel", "parallel", "arbitrary")),
        name="mla_attention",
    )(qt, proj, kn, kr, vt)


def _swa_kernel(q_ref, z0_ref, z1_ref, z2_ref, kc_ref, kp_ref, vc_ref, vp_ref, bias_ref, sink_ref, o_ref,
                kd_ref, vf_ref, vt_ref, s_ref, p_ref, inv_ref, *, tq):
    first = pl.program_id(2) == 0
    r = tq // BLOCK
    low = lax.broadcasted_iota(jnp.int32, (BLOCK, LANES), 1) < B_HEAD_DIM
    for lo, src in ((0, kp_ref), (BLOCK, kc_ref)):
        k = src[...].astype(F32)
        kr = pltpu.roll(k, B_HEAD_DIM, 1)
        half = lax.broadcasted_iota(jnp.int32, k.shape, 1) < B_HEAD_DIM
        kd_ref[0, lo:lo + k.shape[0], :] = jnp.where(half, k, kr).astype(BF16)
        kd_ref[1, lo:lo + k.shape[0], :] = jnp.where(half, kr, k).astype(BF16)
    vf_ref[0:BLOCK, :] = vp_ref[...]
    vf_ref[BLOCK:, :] = vc_ref[...]
    for c in range(r):
        vt = vf_ref[c * BLOCK:(c + 2) * BLOCK, :].astype(F32).T
        for e in range(2):
            ve = vt[e * B_HEAD_DIM:(e + 1) * B_HEAD_DIM]
            vt_ref[e, c] = jnp.concatenate([ve, ve], axis=0).astype(BF16)
    key = lax.broadcasted_iota(jnp.int32, (2 * BLOCK, 2 * BLOCK), 0)
    pen0 = jnp.where((key < BLOCK) & first, NEG, 0.0).astype(F32)
    tiles = [(c, l) for c in range(r) for l in range(B_GROUP)]
    for t, (c, l) in enumerate(tiles):
        qp = q_ref[c * BLOCK:(c + 1) * BLOCK, l * LANES:(l + 1) * LANES]
        zero = jnp.zeros_like(qp)
        q2 = jnp.concatenate([jnp.where(low, qp, zero), jnp.where(low, zero, qp)], axis=0)
        s = _dot_nt(kd_ref[l // 3, c * BLOCK:(c + 2) * BLOCK, :], q2) + bias_ref[0, l]
        s_ref[t] = s + pen0 if c == 0 else s
    for t, (c, l) in enumerate(tiles):
        s = s_ref[t]
        sink = sink_ref[0, l]
        m = jnp.maximum(jnp.max(s, axis=0, keepdims=True), sink)
        p = jnp.exp(s - m)
        inv_ref[t] = 1.0 / (jnp.sum(p, axis=0, keepdims=True) + jnp.exp(sink - m))
        p_ref[t] = p.astype(BF16)
    z_refs = (z0_ref, z1_ref, z2_ref)
    for t, (c, l) in enumerate(tiles):
        rows = slice(c * BLOCK, (c + 1) * BLOCK)
        ot = jnp.dot(vt_ref[l // 3, c], p_ref[t], preferred_element_type=F32) * inv_ref[t]
        o = jnp.concatenate([ot[0:B_HEAD_DIM, 0:BLOCK], ot[B_HEAD_DIM:, BLOCK:]], axis=0).T
        z = z_refs[l // 2][rows, (l % 2) * LANES:(l % 2 + 1) * LANES].astype(F32)
        o_ref[rows, l * LANES:(l + 1) * LANES] = (o * _silu(z)).astype(o_ref.dtype)


def swa_attention(proj, bias, sink, batch, seq, tq=512):
    t = proj.shape[0]
    tq = min(tq, seq)
    nq = seq // tq
    r = tq // BLOCK
    qw = B_GROUP * LANES
    k_off, v_off = 3072 // LANES, 3584 // LANES
    z_off = 5120 // (2 * LANES)

    def prev_map(p, b, i, off):
        return (jnp.maximum(b * (seq // BLOCK) + i * r - 1, b * (seq // BLOCK)), off + p)

    def z_map(p, b, i, part):
        return (b * nq + i, z_off + 3 * p + part)

    return pl.pallas_call(
        functools.partial(_swa_kernel, tq=tq),
        grid=(B_PAIRS, batch, nq),
        in_specs=[pl.BlockSpec((tq, qw), lambda p, b, i: (b * nq + i, p)),
                  pl.BlockSpec((tq, 2 * LANES), functools.partial(z_map, part=0)),
                  pl.BlockSpec((tq, 2 * LANES), functools.partial(z_map, part=1)),
                  pl.BlockSpec((tq, 2 * LANES), functools.partial(z_map, part=2)),
                  pl.BlockSpec((tq, LANES), lambda p, b, i: (b * nq + i, k_off + p)),
                  pl.BlockSpec((BLOCK, LANES), functools.partial(prev_map, off=k_off)),
                  pl.BlockSpec((tq, LANES), lambda p, b, i: (b * nq + i, v_off + p)),
                  pl.BlockSpec((BLOCK, LANES), functools.partial(prev_map, off=v_off)),
                  pl.BlockSpec((1, B_GROUP, 2 * BLOCK, 2 * BLOCK), lambda p, b, i: (p, 0, 0, 0)),
                  pl.BlockSpec((1, B_GROUP, 1, 2 * BLOCK), lambda p, b, i: (p, 0, 0, 0))],
        out_specs=pl.BlockSpec((tq, qw), lambda p, b, i: (b * nq + i, p)),
        out_shape=jax.ShapeDtypeStruct((t, SELF_WIDTH), BF16),
        scratch_shapes=[pltpu.VMEM((2, tq + BLOCK, LANES), BF16), pltpu.VMEM((tq + BLOCK, LANES), BF16),
                        pltpu.VMEM((2, r, LANES, 2 * BLOCK), BF16),
                        pltpu.VMEM((r * B_GROUP, 2 * BLOCK, 2 * BLOCK), F32),
                        pltpu.VMEM((r * B_GROUP, 2 * BLOCK, 2 * BLOCK), BF16),
                        pltpu.VMEM((r * B_GROUP, 1, 2 * BLOCK), F32)],
        compiler_params=_params(("parallel", "parallel", "arbitrary")),
        name="swa_attention",
    )(proj, proj, proj, proj, proj, proj, proj, proj, bias, sink)


def _memattn_kernel(xq_ref, z_ref, mk_ref, mv_ref, o_ref):
    for h in range(X_HEADS):
        cols = slice(h * X_HEAD_DIM, (h + 1) * X_HEAD_DIM)
        s = _dot_nt(xq_ref[:, cols], mk_ref[:, cols])
        m = jnp.max(s, axis=1, keepdims=True)
        p = jnp.exp(s - m)
        l = jnp.sum(p, axis=1, keepdims=True)
        o = jnp.dot(p.astype(BF16), mv_ref[:, cols], preferred_element_type=F32) / l
        o_ref[:, cols] = (o * _silu(z_ref[:, cols].astype(F32))).astype(o_ref.dtype)


def mem_attention(proj, mkv, layer, batch, seq, xq_blk, z_blk, tq=512):
    t = proj.shape[0]
    n_mem = mkv.shape[1] // batch
    tq = min(tq, seq)
    nq = seq // tq
    return pl.pallas_call(
        _memattn_kernel,
        grid=(batch, nq),
        in_specs=[pl.BlockSpec((tq, X_WIDTH), lambda b, i: (b * nq + i, xq_blk)),
                  pl.BlockSpec((tq, X_WIDTH), lambda b, i: (b * nq + i, z_blk)),
                  pl.BlockSpec((None, n_mem, X_WIDTH), lambda b, i: (layer, b, 0)),
                  pl.BlockSpec((None, n_mem, X_WIDTH), lambda b, i: (layer, b, 1))],
        out_specs=pl.BlockSpec((tq, X_WIDTH), lambda b, i: (b * nq + i, 0)),
        out_shape=jax.ShapeDtypeStruct((t, X_WIDTH), BF16),
        compiler_params=_params(("parallel", "arbitrary")),
        name="mem_attention",
    )(proj, proj, mkv, mkv)


def _outproj_kernel(ys_ref, ym_ref, x_ref, ws_ref, wm_ref, o_ref, ob_ref, ss_ref):
    acc = jnp.dot(ys_ref[...], ws_ref[...], preferred_element_type=F32)
    acc = acc + jnp.dot(ym_ref[...], wm_ref[...], preferred_element_type=F32)
    x = x_ref[...] + acc
    o_ref[...] = x
    ob_ref[...] = x.astype(BF16)
    part = jnp.sum(x * x, axis=-1, keepdims=True)

    @pl.when(pl.program_id(1) == 0)
    def _():
        ss_ref[...] = part

    @pl.when(pl.program_id(1) != 0)
    def _():
        ss_ref[...] += part


def out_proj(y_self, y_mem, x, w_out, layer, tm=1024, tn=512):
    t, d = x.shape
    tm = min(tm, t)
    return pl.pallas_call(
        _outproj_kernel,
        grid=(t // tm, d // tn),
        in_specs=[pl.BlockSpec((tm, SELF_WIDTH), lambda i, j: (i, 0)),
                  pl.BlockSpec((tm, X_WIDTH), lambda i, j: (i, 0)),
                  pl.BlockSpec((tm, tn), lambda i, j: (i, j)),
                  pl.BlockSpec((None, SELF_WIDTH, tn), lambda i, j: (layer, 0, j)),
                  pl.BlockSpec((None, X_WIDTH, tn), lambda i, j: (layer, SELF_WIDTH // X_WIDTH, j))],
        out_specs=[pl.BlockSpec((tm, tn), lambda i, j: (i, j)),
                   pl.BlockSpec((tm, tn), lambda i, j: (i, j)),
                   pl.BlockSpec((tm, 1), lambda i, j: (i, 0))],
        out_shape=[jax.ShapeDtypeStruct((t, d), F32), jax.ShapeDtypeStruct((t, d), BF16),
                   jax.ShapeDtypeStruct((t, 1), F32)],
        compiler_params=_params(("parallel", "arbitrary")),
        name="out_proj",
    )(y_self, y_mem, x, w_out, w_out)


A_IN_PAD = 6912


def _relayout_a_in_kernel(wt_ref, g_ref, o_ref):
    o1, o2, o3 = A_Q_RANK, A_Q_RANK + A_KV_RANK, A_Q_RANK + A_KV_RANK + A_ROPE
    o4 = o3 + X_WIDTH
    g = g_ref[0]
    chunk = 512

    def put(dst, src, n, scale):
        for c in range(0, n, chunk):
            x = wt_ref[0, src + c:src + c + chunk, :] * (g * scale)
            o_ref[0, :, dst + c:dst + c + chunk] = x.T.astype(BF16)

    put(0, o4, 4096, 1.0)
    put(4096, 0, A_Q_RANK, 1.0)
    put(5120, o3, X_WIDTH, X_HEAD_DIM ** -0.5)
    put(6144, o1, A_KV_RANK, 1.0)
    half = A_ROPE // 2
    t1 = wt_ref[0, o2:o2 + half, :]
    t2 = wt_ref[0, o2 + half:o3, :]
    kr = jnp.concatenate([t1, t2, t2, t1], axis=0) * g
    o_ref[0, :, 6656:6784] = kr.T.astype(BF16)
    o_ref[0, :, 6784:] = jnp.zeros((o_ref.shape[1], A_IN_PAD - 6784), BF16)


def relayout_a_in(w, g, tk=256):
    n_layers, k, n = w.shape
    return pl.pallas_call(
        _relayout_a_in_kernel,
        grid=(n_layers, k // tk),
        in_specs=[pl.BlockSpec((1, n, tk), lambda a, i: (a, 0, i)),
                  pl.BlockSpec((1, 1, tk), lambda a, i: (a, 0, i))],
        out_specs=pl.BlockSpec((1, tk, A_IN_PAD), lambda a, i: (a, i, 0)),
        out_shape=jax.ShapeDtypeStruct((n_layers, k, A_IN_PAD), BF16),
        compiler_params=_params(("parallel", "parallel")),
        name="relayout_a_w_in",
    )(jnp.swapaxes(w, 1, 2), g.astype(F32)[:, None, :])


def _rope_tables(positions):
    inv = ROPE_THETA ** (-jnp.arange(0, A_ROPE, 2, dtype=F32) / A_ROPE)
    ang = positions.astype(F32).reshape(-1)[:, None] * inv
    cos, sin = jnp.cos(ang), jnp.sin(ang)
    zero = jnp.zeros_like(cos)
    ct = jnp.concatenate([cos, cos, zero, zero], axis=1)
    st = jnp.concatenate([-sin, sin, zero, zero], axis=1)
    return ct, st


def _swap_halves(w):
    half = w.shape[-1] // 2
    return jnp.concatenate([w[..., half:], w[..., :half]], axis=-1)


def _prep_a_w_qb(w):
    n_layers = w.shape[0]
    w = w.astype(BF16).reshape(n_layers, A_Q_RANK, A_HEADS, A_NOPE + A_ROPE)
    rope = w[..., A_NOPE:]
    out = jnp.concatenate([w[..., :A_NOPE], rope, _swap_halves(rope)], axis=-1)
    return out.reshape(n_layers, A_Q_RANK, A_HEADS * A_QPAD).swapaxes(1, 2)


def _prep_a_w_kvb(w):
    n_layers = w.shape[0]
    w = w.astype(BF16).reshape(n_layers, A_KV_RANK, A_HEADS, A_NOPE + A_VDIM)
    wk = w[..., :A_NOPE].reshape(n_layers, A_KV_RANK, A_HEADS * A_NOPE)
    wvt = w[..., A_NOPE:].reshape(n_layers, A_KV_RANK, A_HEADS * A_VDIM).swapaxes(1, 2)
    return wk, wvt


def _prep_b_w_in(w, g):
    nq = B_HEADS * B_HEAD_DIM
    nk = B_KV_HEADS * B_HEAD_DIM
    col = jnp.arange(w.shape[-1])
    scale = jnp.where(col < nq, B_HEAD_DIM ** -0.5,
                      jnp.where((col >= nq + 2 * nk) & (col < nq + 2 * nk + X_WIDTH), X_HEAD_DIM ** -0.5, 1.0))
    return (w * (g.astype(F32)[:, :, None] * scale.astype(F32))).astype(BF16)


def _t5_bucket(dist):
    n = jnp.maximum(dist, 0)
    nf = jnp.maximum(n, 1).astype(F32)
    large = MAX_EXACT + (jnp.log(nf / MAX_EXACT) / math.log(MAX_DIST / MAX_EXACT)
                         * (N_BUCKETS - MAX_EXACT)).astype(jnp.int32)
    large = jnp.minimum(large, N_BUCKETS - 1)
    return jnp.where(n < MAX_EXACT, n, large)


def _swa_tables(rel_bias, sinks):
    q_local = jnp.arange(BLOCK)[None, :]
    k_local = jnp.arange(2 * BLOCK)[:, None]
    dist = q_local + BLOCK - k_local
    in_window = (dist >= 0) & (dist < WINDOW)
    onehot = (_t5_bucket(dist)[:, :, None] == jnp.arange(N_BUCKETS)).astype(F32)
    bias = jnp.einsum("kqn,nh->kqh", onehot, rel_bias.astype(F32), precision=lax.Precision.HIGHEST)
    bias = jnp.where(in_window[:, :, None], bias, NEG)
    bias = bias.reshape(2 * BLOCK, BLOCK, B_PAIRS, B_GROUP, 2)
    bias = bias.transpose(2, 3, 0, 4, 1).reshape(B_PAIRS, B_GROUP, 2 * BLOCK, 2 * BLOCK)
    sink = sinks.astype(F32).reshape(B_PAIRS, B_GROUP, 2, 1)
    sink = jnp.broadcast_to(sink, (B_PAIRS, B_GROUP, 2, BLOCK)).reshape(B_PAIRS, B_GROUP, 1, 2 * BLOCK)
    return bias, sink


def kernel(x, mem, positions, norm_g, mem_norm_g, final_norm_g, w_mem_kv, w_out, a_w_in, a_q_norm_g, a_kv_norm_g,
           a_w_qb, a_w_kvb, b_w_in, b_sinks, rel_bias):
    batch, seq, d = x.shape
    n_mem = mem.shape[1]
    depth = norm_g.shape[0]
    xs = x.reshape(batch * seq, d)
    mems = mem.reshape(batch * n_mem, d)
    ct, st = _rope_tables(positions)
    ctt, stt = ct.T, st.T
    mkv = mem_kv_all(mems, mem_norm_g, w_mem_kv.astype(BF16))
    w_out_b = w_out.astype(BF16)
    a_w_in_b = relayout_a_in(a_w_in, norm_g[0::2])
    b_w_in_b = _prep_b_w_in(b_w_in, norm_g[1::2])
    a_w_qb_b = _prep_a_w_qb(a_w_qb)
    a_wk_b, a_wvt_b = _prep_a_w_kvb(a_w_kvb)

    xb, ss = cast_sumsq(xs)
    for i in range(depth):
        j = i // 2
        if i % 2 == 0:
            proj = norm_matmul(xb, ss, a_w_in_b, j, "mla_in_proj", tn=768)
            qt = mla_q(proj, a_q_norm_g[j], a_w_qb_b, j, ctt, stt, cq_blk=4)
            kn, vt, kr = mla_kv(proj, a_kv_norm_g[j], a_wk_b, a_wvt_b, j, ct, st, ckv_blk=12, kr_blk=52)
            y_self = mla_attention(qt, proj, kn, vt, kr, batch, seq)
            y_mem = mem_attention(proj, mkv, i, batch, seq, xq_blk=5, z_blk=3)
        else:
            proj = norm_matmul(xb, ss, b_w_in_b, j, "swa_in_proj")
            bias, sink = _swa_tables(rel_bias, b_sinks[j])
            y_self = swa_attention(proj, bias, sink, batch, seq)
            y_mem = mem_attention(proj, mkv, i, batch, seq, xq_blk=4, z_blk=8)
        xs, xb, ss = out_proj(y_self, y_mem, xs, w_out_b, i)
    out = rmsnorm_from_sumsq(xs, ss, final_norm_g)
    return out.reshape(batch, seq, d)
```

```python
import functools
import math

import jax
import jax.numpy as jnp
from jax import lax
from jax.experimental import pallas as pl
from jax.experimental.pallas import tpu as pltpu

F32 = jnp.float32
BF16 = jnp.bfloat16

EPS = 1e-6
LANES = 128
NEG = -1e30
VMEM_LIMIT = 56 * 1024 * 1024

X_HEADS = 4
X_HEAD_DIM = 256
X_WIDTH = 1024
SELF_WIDTH = 3072
A_NOPE = 128
A_ROPE = 64
A_VDIM = 128
A_HEADS = 24
A_Q_RANK = 1024
A_KV_RANK = 512
SUM_ROWS = 16
A_QPAD = 256
ROPE_THETA = 10000.0
B_HEAD_DIM = 64
B_HEADS = 48
B_KV_HEADS = 8
B_GROUP = 6
B_PAIRS = B_KV_HEADS // 2
WINDOW = 128
BLOCK = 128
N_BUCKETS = 32
MAX_EXACT = 16
MAX_DIST = 128


def _params(sem):
    return pltpu.CompilerParams(dimension_semantics=sem, vmem_limit_bytes=VMEM_LIMIT)


def _rms(x, g):
    ms = jnp.mean(x * x, axis=-1, keepdims=True)
    return x * lax.rsqrt(ms + EPS) * g


def _silu(z):
    return z * jax.nn.sigmoid(z)


def _dot_nt(a, b):
    return lax.dot_general(a, b, (((1,), (1,)), ((), ())), preferred_element_type=F32)


def _mem_kv_kernel(mem_ref, g_ref, w_ref, o_ref, mn_ref):
    @pl.when(pl.program_id(2) == 0)
    def _():
        mn_ref[...] = _rms(mem_ref[...], g_ref[...]).astype(BF16)

    o_ref[...] = jnp.dot(mn_ref[...], w_ref[...], preferred_element_type=F32).astype(o_ref.dtype)


def mem_kv_all(mem, g, w, tm=512, tn=1024):
    m, d = mem.shape
    n_layers, _, n = w.shape
    tm = min(tm, m)
    return pl.pallas_call(
        _mem_kv_kernel,
        grid=(n_layers, m // tm, n // tn),
        in_specs=[pl.BlockSpec((tm, d), lambda l, i, j: (i, 0)),
                  pl.BlockSpec((None, 1, d), lambda l, i, j: (l, 0, 0)),
                  pl.BlockSpec((None, d, tn), lambda l, i, j: (l, 0, j))],
        out_specs=pl.BlockSpec((None, tm, tn), lambda l, i, j: (l, i, j)),
        out_shape=jax.ShapeDtypeStruct((n_layers, m, n), BF16),
        scratch_shapes=[pltpu.VMEM((tm, d), BF16)],
        compiler_params=_params(("parallel", "parallel", "arbitrary")),
        name="mem_kv_proj",
    )(mem, g.astype(F32)[:, None, :], w)


def _cast_sumsq_kernel(x_ref, xb_ref, ss_ref):
    x = x_ref[...]
    xb_ref[...] = x.astype(BF16)
    ss_ref[...] = jnp.sum(x * x, axis=-1, keepdims=True)


def cast_sumsq(x, tm=256):
    m, d = x.shape
    tm = min(tm, m)
    return pl.pallas_call(
        _cast_sumsq_kernel,
        grid=(m // tm,),
        in_specs=[pl.BlockSpec((tm, d), lambda i: (i, 0))],
        out_specs=[pl.BlockSpec((tm, d), lambda i: (i, 0)), pl.BlockSpec((tm, 1), lambda i: (i, 0))],
        out_shape=[jax.ShapeDtypeStruct((m, d), BF16), jax.ShapeDtypeStruct((m, 1), F32)],
        compiler_params=_params(("parallel",)),
        name="cast_sumsq",
    )(x)


def _norm_matmul_kernel(a_ref, ss_ref, b_ref, o_ref, *, width):
    r = lax.rsqrt(ss_ref[...] * (1.0 / width) + EPS)
    o_ref[...] = (jnp.dot(a_ref[...], b_ref[...], preferred_element_type=F32) * r).astype(o_ref.dtype)


def norm_matmul(a, ss, b, layer, name, tm=1024, tn=1024):
    m, k = a.shape
    n = b.shape[2]
    tm, tn = min(tm, m), min(tn, n)
    return pl.pallas_call(
        functools.partial(_norm_matmul_kernel, width=k),
        grid=(m // tm, n // tn),
        in_specs=[pl.BlockSpec((tm, k), lambda i, j: (i, 0)),
                  pl.BlockSpec((tm, 1), lambda i, j: (i, 0)),
                  pl.BlockSpec((None, k, tn), lambda i, j: (layer, 0, j))],
        out_specs=pl.BlockSpec((tm, tn), lambda i, j: (i, j)),
        out_shape=jax.ShapeDtypeStruct((m, n), BF16),
        compiler_params=_params(("parallel", "arbitrary")),
        name=name,
    )(a, ss, b)


def _scale_rows_kernel(x_ref, ss_ref, g_ref, o_ref):
    x = x_ref[...]
    r = lax.rsqrt(ss_ref[...] * (1.0 / x.shape[-1]) + EPS)
    o_ref[...] = (x * r * g_ref[...]).astype(o_ref.dtype)


def rmsnorm_from_sumsq(x, ss, g, tm=256):
    m, d = x.shape
    tm = min(tm, m)
    return pl.pallas_call(
        _scale_rows_kernel,
        grid=(m // tm,),
        in_specs=[pl.BlockSpec((tm, d), lambda i: (i, 0)),
                  pl.BlockSpec((tm, 1), lambda i: (i, 0)),
                  pl.BlockSpec((1, d), lambda i: (0, 0))],
        out_specs=pl.BlockSpec((tm, d), lambda i: (i, 0)),
        out_shape=jax.ShapeDtypeStruct((m, d), x.dtype),
        compiler_params=_params(("parallel",)),
        name="final_rmsnorm",
    )(x, ss, g.reshape(1, d).astype(F32))


def _rope(u, ct, st):
    return u * ct + pltpu.roll(u, 64, 1) * st


def _qb_kernel(cq_ref, g_ref, wt_ref, ctt_ref, stt_ref, o_ref, cqn_ref, *, heads, scale):
    @pl.when(pl.program_id(1) == 0)
    def _():
        cqn_ref[...] = _rms(cq_ref[...].astype(F32), g_ref[...]).astype(BF16)

    cqn = cqn_ref[...]
    ctt = ctt_ref[...] * scale
    stt = stt_ref[...] * scale
    for h in range(heads):
        lo = h * A_QPAD
        acc = _dot_nt(wt_ref[lo:lo + A_QPAD, :], cqn)
        o_ref[lo:lo + A_NOPE, :] = (acc[:A_NOPE] * scale).astype(BF16)
        u = acc[A_NOPE:]
        swapped = jnp.concatenate([u[2 * A_ROPE // 2:], u[:2 * A_ROPE // 2]], axis=0)
        o_ref[lo + A_NOPE:lo + A_QPAD, :] = (u * ctt + swapped * stt).astype(BF16)


def mla_q(proj, g, w_qbt, layer, ctt, stt, cq_blk, tm=1024, heads=4):
    t = proj.shape[0]
    tm = min(tm, t)
    n = A_HEADS * A_QPAD
    tn = heads * A_QPAD
    scale = (A_NOPE + A_ROPE) ** -0.5 * math.log2(math.e)
    return pl.pallas_call(
        functools.partial(_qb_kernel, heads=heads, scale=scale),
        grid=(t // tm, n // tn),
        in_specs=[pl.BlockSpec((tm, A_Q_RANK), lambda i, j: (i, cq_blk)),
                  pl.BlockSpec((1, A_Q_RANK), lambda i, j: (0, 0)),
                  pl.BlockSpec((None, tn, A_Q_RANK), lambda i, j: (layer, j, 0)),
                  pl.BlockSpec((LANES, tm), lambda i, j: (0, i)),
                  pl.BlockSpec((LANES, tm), lambda i, j: (0, i))],
        out_specs=pl.BlockSpec((tn, tm), lambda i, j: (j, i)),
        out_shape=jax.ShapeDtypeStruct((n, t), BF16),
        scratch_shapes=[pltpu.VMEM((tm, A_Q_RANK), BF16)],
        compiler_params=_params(("parallel", "arbitrary")),
        name="mla_q_up",
    )(proj, g.reshape(1, -1).astype(F32), w_qbt, ctt, stt)


def _kvb_kernel(ckv_ref, g_ref, wk_ref, wvt_ref, kr_ref, ct_ref, st_ref, kn_ref, vt_ref, kro_ref, cn_ref):
    @pl.when(pl.program_id(1) == 0)
    def _():
        cn_ref[...] = _rms(ckv_ref[...].astype(F32), g_ref[...]).astype(BF16)
        kro_ref[...] = _rope(kr_ref[...].astype(F32), ct_ref[...], st_ref[...]).astype(BF16)

    cn = cn_ref[...]
    kn_ref[...] = jnp.dot(cn, wk_ref[...], preferred_element_type=F32).astype(BF16)
    vt_ref[...] = _dot_nt(wvt_ref[...], cn).astype(BF16)


def mla_kv(proj, g, wk, wvt, layer, ct, st, ckv_blk, kr_blk, tm=1024, tn=1024):
    t = proj.shape[0]
    tm = min(tm, t)
    n = wk.shape[2]
    return pl.pallas_call(
        _kvb_kernel,
        grid=(t // tm, n // tn),
        in_specs=[pl.BlockSpec((tm, A_KV_RANK), lambda i, j: (i, ckv_blk)),
                  pl.BlockSpec((1, A_KV_RANK), lambda i, j: (0, 0)),
                  pl.BlockSpec((None, A_KV_RANK, tn), lambda i, j: (layer, 0, j)),
                  pl.BlockSpec((None, tn, A_KV_RANK), lambda i, j: (layer, j, 0)),
                  pl.BlockSpec((tm, LANES), lambda i, j: (i, kr_blk)),
                  pl.BlockSpec((tm, LANES), lambda i, j: (i, 0)),
                  pl.BlockSpec((tm, LANES), lambda i, j: (i, 0))],
        out_specs=[pl.BlockSpec((tm, tn), lambda i, j: (i, j)),
                   pl.BlockSpec((tn, tm), lambda i, j: (j, i)),
                   pl.BlockSpec((tm, LANES), lambda i, j: (i, 0))],
        out_shape=[jax.ShapeDtypeStruct((t, n), BF16),
                   jax.ShapeDtypeStruct((n, t), BF16),
                   jax.ShapeDtypeStruct((t, LANES), BF16)],
        scratch_shapes=[pltpu.VMEM((tm, A_KV_RANK), BF16)],
        compiler_params=_params(("parallel", "arbitrary")),
        name="mla_kv_up",
    )(proj, g.reshape(1, -1).astype(F32), wk, wvt, proj, ct, st)


def _mla_attn_kernel(qt_ref, z_ref, kn_ref, kr_ref, vt_ref, o_ref, sa_ref, sb_ref, m_ref, acc_ref, *,
                     tq, heads):
    qi = pl.program_id(2)
    m_ref[...] = jnp.full(m_ref.shape, NEG, F32)
    acc_ref[...] = jnp.zeros(acc_ref.shape, F32)
    ones = jnp.ones((SUM_ROWS, tq), BF16)

    def scores(j, s_ref):
        rows = pl.ds(pl.multiple_of(j * tq, tq), tq)
        kr = kr_ref[rows, :]
        for h in range(heads):
            k = jnp.concatenate([kn_ref[rows, h * A_NOPE:(h + 1) * A_NOPE], kr], axis=1)
            s_ref[h] = jnp.dot(k, qt_ref[h * A_QPAD:(h + 1) * A_QPAD, :], preferred_element_type=F32)

    def update(j, s_ref, masked):
        rows = pl.ds(pl.multiple_of(j * tq, tq), tq)
        for h in range(heads):
            s = s_ref[h]
            if masked:
                r = lax.broadcasted_iota(jnp.int32, s.shape, 0)
                c = lax.broadcasted_iota(jnp.int32, s.shape, 1)
                s = jnp.where(r <= c, s, NEG)
            m_prev = m_ref[h]
            m_cur = jnp.maximum(m_prev, jnp.max(s, axis=0, keepdims=True))
            alpha = jnp.exp2(m_prev - m_cur)
            p = jnp.exp2(s - m_cur).astype(BF16)
            va = jnp.concatenate([vt_ref[h * A_VDIM:(h + 1) * A_VDIM, rows], ones], axis=0)
            acc_ref[h] = alpha * acc_ref[h] + jnp.dot(va, p, preferred_element_type=F32)
            m_ref[h] = m_cur

    even = (qi % 2) == 1

    @pl.when(even)
    def _():
        scores(0, sb_ref)
        scores(1, sa_ref)
        update(0, sb_ref, False)

    @pl.when(jnp.logical_not(even))
    def _():
        scores(0, sa_ref)

    first = even.astype(jnp.int32)

    def pair(t, carry):
        c = first + 2 * t
        scores(c + 1, sb_ref)
        update(c, sa_ref, False)
        scores(c + 2, sa_ref)
        update(c + 1, sb_ref, False)
        return carry

    lax.fori_loop(0, (qi - first) // 2, pair, 0)
    update(qi, sa_ref, True)
    for h in range(heads):
        acc = acc_ref[h]
        o = (acc[:A_VDIM] * (1.0 / acc[A_VDIM:A_VDIM + 1])).T
        gate = _silu(z_ref[:, h * A_VDIM:(h + 1) * A_VDIM].astype(F32))
        o_ref[:, h * A_VDIM:(h + 1) * A_VDIM] = (o * gate).astype(o_ref.dtype)


def mla_attention(qt, proj, kn, vt, kr, batch, seq, tq=512, heads=6):
    t = proj.shape[0]
    tq = min(tq, seq)
    nq = seq // tq
    return pl.pallas_call(
        functools.partial(_mla_attn_kernel, tq=tq, heads=heads),
        grid=(batch, A_HEADS // heads, nq),
        in_specs=[pl.BlockSpec((heads * A_QPAD, tq), lambda b, h, i: (h, b * nq + i)),
                  pl.BlockSpec((tq, heads * A_VDIM), lambda b, h, i: (b * nq + i, h)),
                  pl.BlockSpec((seq, heads * A_NOPE), lambda b, h, i: (b, h)),
                  pl.BlockSpec((seq, LANES), lambda b, h, i: (b, 0)),
                  pl.BlockSpec((heads * A_VDIM, seq), lambda b, h, i: (h, b))],
        out_specs=pl.BlockSpec((tq, heads * A_VDIM), lambda b, h, i: (b * nq + i, h)),
        out_shape=jax.ShapeDtypeStruct((t, A_HEADS * A_VDIM), BF16),
        scratch_shapes=[pltpu.VMEM((heads, tq, tq), F32), pltpu.VMEM((heads, tq, tq), F32),
                        pltpu.VMEM((heads, 1, tq), F32),
                        pltpu.VMEM((heads, A_VDIM + SUM_ROWS, tq), F32)],
        compiler_params=_params(("parallel", "parallel", "arbitrary")),
        name="mla_attention",
    )(qt, proj, kn, kr, vt)


def _swa_kernel(q_ref, z0_ref, z1_ref, z2_ref, kc_ref, kp_ref, vc_ref, vp_ref, bias_ref, sink_ref, o_ref,
                kd_ref, vf_ref, vt_ref, s_ref, p_ref, es_ref, *, tq):
    first = pl.program_id(2) == 0
    r = tq // BLOCK
    low = lax.broadcasted_iota(jnp.int32, (BLOCK, LANES), 1) < B_HEAD_DIM
    for lo, src in ((0, kp_ref), (BLOCK, kc_ref)):
        k = src[...].astype(F32)
        kr = pltpu.roll(k, B_HEAD_DIM, 1)
        half = lax.broadcasted_iota(jnp.int32, k.shape, 1) < B_HEAD_DIM
        kd_ref[0, lo:lo + k.shape[0], :] = jnp.where(half, k, kr).astype(BF16)
        kd_ref[1, lo:lo + k.shape[0], :] = jnp.where(half, kr, k).astype(BF16)
    vf_ref[0:BLOCK, :] = vp_ref[...]
    vf_ref[BLOCK:, :] = vc_ref[...]
    ones = jnp.ones((SUM_ROWS, 2 * BLOCK), BF16)
    for c in range(r):
        vt = vf_ref[c * BLOCK:(c + 2) * BLOCK, :].astype(F32).T
        for e in range(2):
            ve = vt[e * B_HEAD_DIM:(e + 1) * B_HEAD_DIM].astype(BF16)
            vt_ref[e, c] = jnp.concatenate([ve, ve, ones], axis=0)
    slot = lax.broadcasted_iota(jnp.int32, (BLOCK, 2 * BLOCK), 0)
    query = lax.broadcasted_iota(jnp.int32, (BLOCK, 2 * BLOCK), 1) & (BLOCK - 1)
    from_prev = slot > query
    pen0 = jnp.where(from_prev & first, NEG, 0.0).astype(F32)
    tiles = [(c, l) for c in range(r) for l in range(B_GROUP)]
    for t, (c, l) in enumerate(tiles):
        qp = q_ref[c * BLOCK:(c + 1) * BLOCK, l * LANES:(l + 1) * LANES]
        zero = jnp.zeros_like(qp)
        q2 = jnp.concatenate([jnp.where(low, qp, zero), jnp.where(low, zero, qp)], axis=0)
        sf = _dot_nt(kd_ref[l // 3, c * BLOCK:(c + 2) * BLOCK, :], q2)
        s = jnp.where(from_prev, sf[:BLOCK], sf[BLOCK:]) + bias_ref[0, l]
        s_ref[t] = s + pen0 if c == 0 else s
    for t, (c, l) in enumerate(tiles):
        s = s_ref[t]
        sink = sink_ref[0, l]
        m = jnp.maximum(jnp.max(s, axis=0, keepdims=True), sink)
        p = jnp.exp(s - m)
        es_ref[t] = jnp.exp(sink - m)
        zero = jnp.zeros_like(p)
        p_ref[t, 0:BLOCK] = jnp.where(from_prev, p, zero).astype(BF16)
        p_ref[t, BLOCK:] = jnp.where(from_prev, zero, p).astype(BF16)
    z_refs = (z0_ref, z1_ref, z2_ref)
    for t, (c, l) in enumerate(tiles):
        rows = slice(c * BLOCK, (c + 1) * BLOCK)
        ot = jnp.dot(vt_ref[l // 3, c], p_ref[t], preferred_element_type=F32)
        ot = ot[:LANES] * (1.0 / (ot[LANES:LANES + 1] + es_ref[t]))
        o = jnp.concatenate([ot[0:B_HEAD_DIM, 0:BLOCK], ot[B_HEAD_DIM:, BLOCK:]], axis=0).T
        z = z_refs[l // 2][rows, (l % 2) * LANES:(l % 2 + 1) * LANES].astype(F32)
        o_ref[rows, l * LANES:(l + 1) * LANES] = (o * _silu(z)).astype(o_ref.dtype)


def swa_attention(proj, bias, sink, batch, seq, tq=512):
    t = proj.shape[0]
    tq = min(tq, seq)
    nq = seq // tq
    r = tq // BLOCK
    qw = B_GROUP * LANES
    k_off, v_off = 3072 // LANES, 3584 // LANES
    z_off = 5120 // (2 * LANES)

    def prev_map(p, b, i, off):
        return (jnp.maximum(b * (seq // BLOCK) + i * r - 1, b * (seq // BLOCK)), off + p)

    def z_map(p, b, i, part):
        return (b * nq + i, z_off + 3 * p + part)

    return pl.pallas_call(
        functools.partial(_swa_kernel, tq=tq),
        grid=(B_PAIRS, batch, nq),
        in_specs=[pl.BlockSpec((tq, qw), lambda p, b, i: (b * nq + i, p)),
                  pl.BlockSpec((tq, 2 * LANES), functools.partial(z_map, part=0)),
                  pl.BlockSpec((tq, 2 * LANES), functools.partial(z_map, part=1)),
                  pl.BlockSpec((tq, 2 * LANES), functools.partial(z_map, part=2)),
                  pl.BlockSpec((tq, LANES), lambda p, b, i: (b * nq + i, k_off + p)),
                  pl.BlockSpec((BLOCK, LANES), functools.partial(prev_map, off=k_off)),
                  pl.BlockSpec((tq, LANES), lambda p, b, i: (b * nq + i, v_off + p)),
                  pl.BlockSpec((BLOCK, LANES), functools.partial(prev_map, off=v_off)),
                  pl.BlockSpec((1, B_GROUP, BLOCK, 2 * BLOCK), lambda p, b, i: (p, 0, 0, 0)),
                  pl.BlockSpec((1, B_GROUP, 1, 2 * BLOCK), lambda p, b, i: (p, 0, 0, 0))],
        out_specs=pl.BlockSpec((tq, qw), lambda p, b, i: (b * nq + i, p)),
        out_shape=jax.ShapeDtypeStruct((t, SELF_WIDTH), BF16),
        scratch_shapes=[pltpu.VMEM((2, tq + BLOCK, LANES), BF16), pltpu.VMEM((tq + BLOCK, LANES), BF16),
                        pltpu.VMEM((2, r, LANES + SUM_ROWS, 2 * BLOCK), BF16),
                        pltpu.VMEM((r * B_GROUP, BLOCK, 2 * BLOCK), F32),
                        pltpu.VMEM((r * B_GROUP, 2 * BLOCK, 2 * BLOCK), BF16),
                        pltpu.VMEM((r * B_GROUP, 1, 2 * BLOCK), F32)],
        compiler_params=_params(("parallel", "parallel", "arbitrary")),
        name="swa_attention",
    )(proj, proj, proj, proj, proj, proj, proj, proj, bias, sink)


def _memattn_kernel(xq_ref, z_ref, mk_ref, mv_ref, o_ref):
    for h in range(X_HEADS):
        cols = slice(h * X_HEAD_DIM, (h + 1) * X_HEAD_DIM)
        s = _dot_nt(xq_ref[:, cols], mk_ref[:, cols])
        m = jnp.max(s, axis=1, keepdims=True)
        p = jnp.exp(s - m)
        l = jnp.sum(p, axis=1, keepdims=True)
        o = jnp.dot(p.astype(BF16), mv_ref[:, cols], preferred_element_type=F32) / l
        o_ref[:, cols] = (o * _silu(z_ref[:, cols].astype(F32))).astype(o_ref.dtype)


def mem_attention(proj, mkv, layer, batch, seq, xq_blk, z_blk, tq=512):
    t = proj.shape[0]
    n_mem = mkv.shape[1] // batch
    tq = min(tq, seq)
    nq = seq // tq
    return pl.pallas_call(
        _memattn_kernel,
        grid=(batch, nq),
        in_specs=[pl.BlockSpec((tq, X_WIDTH), lambda b, i: (b * nq + i, xq_blk)),
                  pl.BlockSpec((tq, X_WIDTH), lambda b, i: (b * nq + i, z_blk)),
                  pl.BlockSpec((None, n_mem, X_WIDTH), lambda b, i: (layer, b, 0)),
                  pl.BlockSpec((None, n_mem, X_WIDTH), lambda b, i: (layer, b, 1))],
        out_specs=pl.BlockSpec((tq, X_WIDTH), lambda b, i: (b * nq + i, 0)),
        out_shape=jax.ShapeDtypeStruct((t, X_WIDTH), BF16),
        compiler_params=_params(("parallel", "arbitrary")),
        name="mem_attention",
    )(proj, proj, mkv, mkv)


def _outproj_kernel(ys_ref, ym_ref, x_ref, ws_ref, wm_ref, o_ref, ob_ref, ss_ref):
    acc = jnp.dot(ys_ref[...], ws_ref[...], preferred_element_type=F32)
    acc = acc + jnp.dot(ym_ref[...], wm_ref[...], preferred_element_type=F32)
    x = x_ref[...] + acc
    o_ref[...] = x
    ob_ref[...] = x.astype(BF16)
    part = jnp.sum(x * x, axis=-1, keepdims=True)

    @pl.when(pl.program_id(1) == 0)
    def _():
        ss_ref[...] = part

    @pl.when(pl.program_id(1) != 0)
    def _():
        ss_ref[...] += part


def out_proj(y_self, y_mem, x, w_out, layer, tm=1024, tn=512):
    t, d = x.shape
    tm = min(tm, t)
    return pl.pallas_call(
        _outproj_kernel,
        grid=(t // tm, d // tn),
        in_specs=[pl.BlockSpec((tm, SELF_WIDTH), lambda i, j: (i, 0)),
                  pl.BlockSpec((tm, X_WIDTH), lambda i, j: (i, 0)),
                  pl.BlockSpec((tm, tn), lambda i, j: (i, j)),
                  pl.BlockSpec((None, SELF_WIDTH, tn), lambda i, j: (layer, 0, j)),
                  pl.BlockSpec((None, X_WIDTH, tn), lambda i, j: (layer, SELF_WIDTH // X_WIDTH, j))],
        out_specs=[pl.BlockSpec((tm, tn), lambda i, j: (i, j)),
                   pl.BlockSpec((tm, tn), lambda i, j: (i, j)),
                   pl.BlockSpec((tm, 1), lambda i, j: (i, 0))],
        out_shape=[jax.ShapeDtypeStruct((t, d), F32), jax.ShapeDtypeStruct((t, d), BF16),
                   jax.ShapeDtypeStruct((t, 1), F32)],
        compiler_params=_params(("parallel", "arbitrary")),
        name="out_proj",
    )(y_self, y_mem, x, w_out, w_out)


A_IN_PAD = 6912


def _relayout_a_in_kernel(wt_ref, g_ref, o_ref):
    o1, o2, o3 = A_Q_RANK, A_Q_RANK + A_KV_RANK, A_Q_RANK + A_KV_RANK + A_ROPE
    o4 = o3 + X_WIDTH
    g = g_ref[0]
    chunk = 512

    def put(dst, src, n, scale):
        for c in range(0, n, chunk):
            x = wt_ref[0, src + c:src + c + chunk, :] * (g * scale)
            o_ref[0, :, dst + c:dst + c + chunk] = x.T.astype(BF16)

    put(0, o4, 4096, 1.0)
    put(4096, 0, A_Q_RANK, 1.0)
    put(5120, o3, X_WIDTH, X_HEAD_DIM ** -0.5)
    put(6144, o1, A_KV_RANK, 1.0)
    half = A_ROPE // 2
    t1 = wt_ref[0, o2:o2 + half, :]
    t2 = wt_ref[0, o2 + half:o3, :]
    kr = jnp.concatenate([t1, t2, t2, t1], axis=0) * g
    o_ref[0, :, 6656:6784] = kr.T.astype(BF16)
    o_ref[0, :, 6784:] = jnp.zeros((o_ref.shape[1], A_IN_PAD - 6784), BF16)


def relayout_a_in(w, g, tk=256):
    n_layers, k, n = w.shape
    return pl.pallas_call(
        _relayout_a_in_kernel,
        grid=(n_layers, k // tk),
        in_specs=[pl.BlockSpec((1, n, tk), lambda a, i: (a, 0, i)),
                  pl.BlockSpec((1, 1, tk), lambda a, i: (a, 0, i))],
        out_specs=pl.BlockSpec((1, tk, A_IN_PAD), lambda a, i: (a, i, 0)),
        out_shape=jax.ShapeDtypeStruct((n_layers, k, A_IN_PAD), BF16),
        compiler_params=_params(("parallel", "parallel")),
        name="relayout_a_w_in",
    )(jnp.swapaxes(w, 1, 2), g.astype(F32)[:, None, :])


def _rope_tables(positions):
    inv = ROPE_THETA ** (-jnp.arange(0, A_ROPE, 2, dtype=F32) / A_ROPE)
    ang = positions.astype(F32).reshape(-1)[:, None] * inv
    cos, sin = jnp.cos(ang), jnp.sin(ang)
    zero = jnp.zeros_like(cos)
    ct = jnp.concatenate([cos, cos, zero, zero], axis=1)
    st = jnp.concatenate([-sin, sin, zero, zero], axis=1)
    return ct, st


def _swap_halves(w):
    half = w.shape[-1] // 2
    return jnp.concatenate([w[..., half:], w[..., :half]], axis=-1)


def _prep_a_w_qb(w):
    n_layers = w.shape[0]
    w = w.astype(BF16).reshape(n_layers, A_Q_RANK, A_HEADS, A_NOPE + A_ROPE)
    rope = w[..., A_NOPE:]
    out = jnp.concatenate([w[..., :A_NOPE], rope, _swap_halves(rope)], axis=-1)
    return out.reshape(n_layers, A_Q_RANK, A_HEADS * A_QPAD).swapaxes(1, 2)


def _prep_a_w_kvb(w):
    n_layers = w.shape[0]
    w = w.astype(BF16).reshape(n_layers, A_KV_RANK, A_HEADS, A_NOPE + A_VDIM)
    wk = w[..., :A_NOPE].reshape(n_layers, A_KV_RANK, A_HEADS * A_NOPE)
    wvt = w[..., A_NOPE:].reshape(n_layers, A_KV_RANK, A_HEADS * A_VDIM).swapaxes(1, 2)
    return wk, wvt


def _prep_b_w_in(w, g):
    nq = B_HEADS * B_HEAD_DIM
    nk = B_KV_HEADS * B_HEAD_DIM
    col = jnp.arange(w.shape[-1])
    scale = jnp.where(col < nq, B_HEAD_DIM ** -0.5,
                      jnp.where((col >= nq + 2 * nk) & (col < nq + 2 * nk + X_WIDTH), X_HEAD_DIM ** -0.5, 1.0))
    return (w * (g.astype(F32)[:, :, None] * scale.astype(F32))).astype(BF16)


def _t5_bucket(dist):
    n = jnp.maximum(dist, 0)
    nf = jnp.maximum(n, 1).astype(F32)
    large = MAX_EXACT + (jnp.log(nf / MAX_EXACT) / math.log(MAX_DIST / MAX_EXACT)
                         * (N_BUCKETS - MAX_EXACT)).astype(jnp.int32)
    large = jnp.minimum(large, N_BUCKETS - 1)
    return jnp.where(n < MAX_EXACT, n, large)


def _swa_tables(rel_bias, sinks):
    q_local = jnp.arange(BLOCK)[None, :]
    slot = jnp.arange(BLOCK)[:, None]
    dist = jnp.where(slot > q_local, q_local + BLOCK - slot, q_local - slot)
    onehot = (_t5_bucket(dist)[:, :, None] == jnp.arange(N_BUCKETS)).astype(F32)
    bias = jnp.einsum("kqn,nh->kqh", onehot, rel_bias.astype(F32), precision=lax.Precision.HIGHEST)
    bias = bias.reshape(BLOCK, BLOCK, B_PAIRS, B_GROUP, 2)
    bias = bias.transpose(2, 3, 0, 4, 1).reshape(B_PAIRS, B_GROUP, BLOCK, 2 * BLOCK)
    sink = sinks.astype(F32).reshape(B_PAIRS, B_GROUP, 2, 1)
    sink = jnp.broadcast_to(sink, (B_PAIRS, B_GROUP, 2, BLOCK)).reshape(B_PAIRS, B_GROUP, 1, 2 * BLOCK)
    return bias, sink


def kernel(x, mem, positions, norm_g, mem_norm_g, final_norm_g, w_mem_kv, w_out, a_w_in, a_q_norm_g, a_kv_norm_g,
           a_w_qb, a_w_kvb, b_w_in, b_sinks, rel_bias):
    batch, seq, d = x.shape
    n_mem = mem.shape[1]
    depth = norm_g.shape[0]
    xs = x.reshape(batch * seq, d)
    mems = mem.reshape(batch * n_mem, d)
    ct, st = _rope_tables(positions)
    ctt, stt = ct.T, st.T
    mkv = mem_kv_all(mems, mem_norm_g, w_mem_kv.astype(BF16))
    w_out_b = w_out.astype(BF16)
    a_w_in_b = relayout_a_in(a_w_in, norm_g[0::2])
    b_w_in_b = _prep_b_w_in(b_w_in, norm_g[1::2])
    a_w_qb_b = _prep_a_w_qb(a_w_qb)
    a_wk_b, a_wvt_b = _prep_a_w_kvb(a_w_kvb)

    xb, ss = cast_sumsq(xs)
    for i in range(depth):
        j = i // 2
        if i % 2 == 0:
            proj = norm_matmul(xb, ss, a_w_in_b, j, "mla_in_proj", tn=768)
            qt = mla_q(proj, a_q_norm_g[j], a_w_qb_b, j, ctt, stt, cq_blk=4)
            kn, vt, kr = mla_kv(proj, a_kv_norm_g[j], a_wk_b, a_wvt_b, j, ct, st, ckv_blk=12, kr_blk=52)
            y_self = mla_attention(qt, proj, kn, vt, kr, batch, seq)
            y_mem = mem_attention(proj, mkv, i, batch, seq, xq_blk=5, z_blk=3)
        else:
            proj = norm_matmul(xb, ss, b_w_in_b, j, "swa_in_proj")
            bias, sink = _swa_tables(rel_bias, b_sinks[j])
            y_self = swa_attention(proj, bias, sink, batch, seq)
            y_mem = mem_attention(proj, mkv, i, batch, seq, xq_blk=4, z_blk=8)
        xs, xb, ss = out_proj(y_self, y_mem, xs, w_out_b, i)
    out = rmsnorm_from_sumsq(xs, ss, final_norm_g)
    return out.reshape(batch, seq, d)
```

```python
import functools
import math

import jax
import jax.numpy as jnp
from jax import lax
from jax.experimental import pallas as pl
from jax.experimental.pallas import tpu as pltpu

F32 = jnp.float32
BF16 = jnp.bfloat16

EPS = 1e-6
LANES = 128
NEG = -1e30
VMEM_LIMIT = 56 * 1024 * 1024

X_HEADS = 4
X_HEAD_DIM = 256
X_WIDTH = 1024
SELF_WIDTH = 3072
A_NOPE = 128
A_ROPE = 64
A_VDIM = 128
A_HEADS = 24
A_Q_RANK = 1024
A_KV_RANK = 512
SUM_ROWS = 16
A_QPAD = 256
ROPE_THETA = 10000.0
B_HEAD_DIM = 64
B_HEADS = 48
B_KV_HEADS = 8
B_GROUP = 6
B_PAIRS = B_KV_HEADS // 2
WINDOW = 128
BLOCK = 128
N_BUCKETS = 32
MAX_EXACT = 16
MAX_DIST = 128


def _params(sem):
    return pltpu.CompilerParams(dimension_semantics=sem, vmem_limit_bytes=VMEM_LIMIT)


def _rms(x, g):
    ms = jnp.mean(x * x, axis=-1, keepdims=True)
    return x * lax.rsqrt(ms + EPS) * g


def _silu(z):
    return z * jax.nn.sigmoid(z)


def _dot_nt(a, b):
    return lax.dot_general(a, b, (((1,), (1,)), ((), ())), preferred_element_type=F32)


def _mem_kv_kernel(mem_ref, g_ref, w_ref, o_ref, mn_ref):
    @pl.when(pl.program_id(2) == 0)
    def _():
        mn_ref[...] = _rms(mem_ref[...], g_ref[...]).astype(BF16)

    o_ref[...] = jnp.dot(mn_ref[...], w_ref[...], preferred_element_type=F32).astype(o_ref.dtype)


def mem_kv_all(mem, g, w, tm=512, tn=1024):
    m, d = mem.shape
    n_layers, _, n = w.shape
    tm = min(tm, m)
    return pl.pallas_call(
        _mem_kv_kernel,
        grid=(n_layers, m // tm, n // tn),
        in_specs=[pl.BlockSpec((tm, d), lambda l, i, j: (i, 0)),
                  pl.BlockSpec((None, 1, d), lambda l, i, j: (l, 0, 0)),
                  pl.BlockSpec((None, d, tn), lambda l, i, j: (l, 0, j))],
        out_specs=pl.BlockSpec((None, tm, tn), lambda l, i, j: (l, i, j)),
        out_shape=jax.ShapeDtypeStruct((n_layers, m, n), BF16),
        scratch_shapes=[pltpu.VMEM((tm, d), BF16)],
        compiler_params=_params(("parallel", "parallel", "arbitrary")),
        name="mem_kv_proj",
    )(mem, g.astype(F32)[:, None, :], w)


def _cast_sumsq_kernel(x_ref, xb_ref, ss_ref):
    x = x_ref[...]
    xb_ref[...] = x.astype(BF16)
    ss_ref[...] = jnp.sum(x * x, axis=-1, keepdims=True)


def cast_sumsq(x, tm=256):
    m, d = x.shape
    tm = min(tm, m)
    return pl.pallas_call(
        _cast_sumsq_kernel,
        grid=(m // tm,),
        in_specs=[pl.BlockSpec((tm, d), lambda i: (i, 0))],
        out_specs=[pl.BlockSpec((tm, d), lambda i: (i, 0)), pl.BlockSpec((tm, 1), lambda i: (i, 0))],
        out_shape=[jax.ShapeDtypeStruct((m, d), BF16), jax.ShapeDtypeStruct((m, 1), F32)],
        compiler_params=_params(("parallel",)),
        name="cast_sumsq",
    )(x)


def _norm_matmul_kernel(a_ref, ss_ref, b_ref, o_ref, *, width):
    r = lax.rsqrt(ss_ref[...] * (1.0 / width) + EPS)
    o_ref[...] = (jnp.dot(a_ref[...], b_ref[...], preferred_element_type=F32) * r).astype(o_ref.dtype)


def norm_matmul(a, ss, b, layer, name, tm=1024, tn=1024):
    m, k = a.shape
    n = b.shape[2]
    tm, tn = min(tm, m), min(tn, n)
    return pl.pallas_call(
        functools.partial(_norm_matmul_kernel, width=k),
        grid=(m // tm, n // tn),
        in_specs=[pl.BlockSpec((tm, k), lambda i, j: (i, 0)),
                  pl.BlockSpec((tm, 1), lambda i, j: (i, 0)),
                  pl.BlockSpec((None, k, tn), lambda i, j: (layer, 0, j))],
        out_specs=pl.BlockSpec((tm, tn), lambda i, j: (i, j)),
        out_shape=jax.ShapeDtypeStruct((m, n), BF16),
        compiler_params=_params(("parallel", "arbitrary")),
        name=name,
    )(a, ss, b)


def _scale_rows_kernel(x_ref, ss_ref, g_ref, o_ref):
    x = x_ref[...]
    r = lax.rsqrt(ss_ref[...] * (1.0 / x.shape[-1]) + EPS)
    o_ref[...] = (x * r * g_ref[...]).astype(o_ref.dtype)


def rmsnorm_from_sumsq(x, ss, g, tm=256):
    m, d = x.shape
    tm = min(tm, m)
    return pl.pallas_call(
        _scale_rows_kernel,
        grid=(m // tm,),
        in_specs=[pl.BlockSpec((tm, d), lambda i: (i, 0)),
                  pl.BlockSpec((tm, 1), lambda i: (i, 0)),
                  pl.BlockSpec((1, d), lambda i: (0, 0))],
        out_specs=pl.BlockSpec((tm, d), lambda i: (i, 0)),
        out_shape=jax.ShapeDtypeStruct((m, d), x.dtype),
        compiler_params=_params(("parallel",)),
        name="final_rmsnorm",
    )(x, ss, g.reshape(1, d).astype(F32))


def _rope(u, ct, st):
    return u * ct + pltpu.roll(u, 64, 1) * st


def _qb_kernel(cq_ref, g_ref, wt_ref, ctt_ref, stt_ref, o_ref, cqn_ref, *, heads, scale):
    @pl.when(pl.program_id(1) == 0)
    def _():
        cqn_ref[...] = _rms(cq_ref[...].astype(F32), g_ref[...]).astype(BF16)

    cqn = cqn_ref[...]
    ctt = ctt_ref[...] * scale
    stt = stt_ref[...] * scale
    for h in range(heads):
        lo = h * A_QPAD
        acc = _dot_nt(wt_ref[lo:lo + A_QPAD, :], cqn)
        o_ref[lo:lo + A_NOPE, :] = (acc[:A_NOPE] * scale).astype(BF16)
        u = acc[A_NOPE:]
        swapped = jnp.concatenate([u[2 * A_ROPE // 2:], u[:2 * A_ROPE // 2]], axis=0)
        o_ref[lo + A_NOPE:lo + A_QPAD, :] = (u * ctt + swapped * stt).astype(BF16)


def mla_q(proj, g, w_qbt, layer, ctt, stt, cq_blk, tm=1024, heads=6):
    t = proj.shape[0]
    tm = min(tm, t)
    n = A_HEADS * A_QPAD
    tn = heads * A_QPAD
    scale = (A_NOPE + A_ROPE) ** -0.5 * math.log2(math.e)
    return pl.pallas_call(
        functools.partial(_qb_kernel, heads=heads, scale=scale),
        grid=(t // tm, n // tn),
        in_specs=[pl.BlockSpec((tm, A_Q_RANK), lambda i, j: (i, cq_blk)),
                  pl.BlockSpec((1, A_Q_RANK), lambda i, j: (0, 0)),
                  pl.BlockSpec((None, tn, A_Q_RANK), lambda i, j: (layer, j, 0)),
                  pl.BlockSpec((LANES, tm), lambda i, j: (0, i)),
                  pl.BlockSpec((LANES, tm), lambda i, j: (0, i))],
        out_specs=pl.BlockSpec((tn, tm), lambda i, j: (j, i)),
        out_shape=jax.ShapeDtypeStruct((n, t), BF16),
        scratch_shapes=[pltpu.VMEM((tm, A_Q_RANK), BF16)],
        compiler_params=_params(("parallel", "arbitrary")),
        name="mla_q_up",
    )(proj, g.reshape(1, -1).astype(F32), w_qbt, ctt, stt)


def _kvb_kernel(ckv_ref, g_ref, wk_ref, wvt_ref, kr_ref, ct_ref, st_ref, kn_ref, vt_ref, kro_ref, cn_ref):
    @pl.when(pl.program_id(1) == 0)
    def _():
        cn_ref[...] = _rms(ckv_ref[...].astype(F32), g_ref[...]).astype(BF16)
        kro_ref[...] = _rope(kr_ref[...].astype(F32), ct_ref[...], st_ref[...]).astype(BF16)

    cn = cn_ref[...]
    kn_ref[...] = jnp.dot(cn, wk_ref[...], preferred_element_type=F32).astype(BF16)
    vt_ref[...] = _dot_nt(wvt_ref[...], cn).astype(BF16)


def mla_kv(proj, g, wk, wvt, layer, ct, st, ckv_blk, kr_blk, tm=1024, tn=1536):
    t = proj.shape[0]
    tm = min(tm, t)
    n = wk.shape[2]
    return pl.pallas_call(
        _kvb_kernel,
        grid=(t // tm, n // tn),
        in_specs=[pl.BlockSpec((tm, A_KV_RANK), lambda i, j: (i, ckv_blk)),
                  pl.BlockSpec((1, A_KV_RANK), lambda i, j: (0, 0)),
                  pl.BlockSpec((None, A_KV_RANK, tn), lambda i, j: (layer, 0, j)),
                  pl.BlockSpec((None, tn, A_KV_RANK), lambda i, j: (layer, j, 0)),
                  pl.BlockSpec((tm, LANES), lambda i, j: (i, kr_blk)),
                  pl.BlockSpec((tm, LANES), lambda i, j: (i, 0)),
                  pl.BlockSpec((tm, LANES), lambda i, j: (i, 0))],
        out_specs=[pl.BlockSpec((tm, tn), lambda i, j: (i, j)),
                   pl.BlockSpec((tn, tm), lambda i, j: (j, i)),
                   pl.BlockSpec((tm, LANES), lambda i, j: (i, 0))],
        out_shape=[jax.ShapeDtypeStruct((t, n), BF16),
                   jax.ShapeDtypeStruct((n, t), BF16),
                   jax.ShapeDtypeStruct((t, LANES), BF16)],
        scratch_shapes=[pltpu.VMEM((tm, A_KV_RANK), BF16)],
        compiler_params=_params(("parallel", "arbitrary")),
        name="mla_kv_up",
    )(proj, g.reshape(1, -1).astype(F32), wk, wvt, proj, ct, st)


def _mla_attn_kernel(qt_ref, z_ref, kn_ref, kr_ref, vt_ref, o_ref, sa_ref, sb_ref, m_ref, acc_ref, *,
                     tq, heads):
    qi = pl.program_id(2)
    m_ref[...] = jnp.full(m_ref.shape, NEG, F32)
    acc_ref[...] = jnp.zeros(acc_ref.shape, F32)
    ones = jnp.ones((SUM_ROWS, tq), BF16)

    def scores(j, s_ref):
        rows = pl.ds(pl.multiple_of(j * tq, tq), tq)
        kr = kr_ref[rows, :]
        for h in range(heads):
            k = jnp.concatenate([kn_ref[rows, h * A_NOPE:(h + 1) * A_NOPE], kr], axis=1)
            s_ref[h] = jnp.dot(k, qt_ref[h * A_QPAD:(h + 1) * A_QPAD, :], preferred_element_type=F32)

    def update(j, s_ref, masked):
        rows = pl.ds(pl.multiple_of(j * tq, tq), tq)
        for h in range(heads):
            s = s_ref[h]
            if masked:
                r = lax.broadcasted_iota(jnp.int32, s.shape, 0)
                c = lax.broadcasted_iota(jnp.int32, s.shape, 1)
                s = jnp.where(r <= c, s, NEG)
            m_prev = m_ref[h]
            m_cur = jnp.maximum(m_prev, jnp.max(s, axis=0, keepdims=True))
            alpha = jnp.exp2(m_prev - m_cur)
            p = jnp.exp2(s - m_cur).astype(BF16)
            va = jnp.concatenate([vt_ref[h * A_VDIM:(h + 1) * A_VDIM, rows], ones], axis=0)
            acc_ref[h] = alpha * acc_ref[h] + jnp.dot(va, p, preferred_element_type=F32)
            m_ref[h] = m_cur

    even = (qi % 2) == 1

    @pl.when(even)
    def _():
        scores(0, sb_ref)
        scores(1, sa_ref)
        update(0, sb_ref, False)

    @pl.when(jnp.logical_not(even))
    def _():
        scores(0, sa_ref)

    first = even.astype(jnp.int32)

    def pair(t, carry):
        c = first + 2 * t
        scores(c + 1, sb_ref)
        update(c, sa_ref, False)
        scores(c + 2, sa_ref)
        update(c + 1, sb_ref, False)
        return carry

    lax.fori_loop(0, (qi - first) // 2, pair, 0)
    update(qi, sa_ref, True)
    for h in range(heads):
        acc = acc_ref[h]
        o = (acc[:A_VDIM] * (1.0 / acc[A_VDIM:A_VDIM + 1])).T
        gate = _silu(z_ref[:, h * A_VDIM:(h + 1) * A_VDIM].astype(F32))
        o_ref[:, h * A_VDIM:(h + 1) * A_VDIM] = (o * gate).astype(o_ref.dtype)


def mla_attention(qt, proj, kn, vt, kr, batch, seq, tq=512, heads=6):
    t = proj.shape[0]
    tq = min(tq, seq)
    nq = seq // tq
    return pl.pallas_call(
        functools.partial(_mla_attn_kernel, tq=tq, heads=heads),
        grid=(batch, A_HEADS // heads, nq),
        in_specs=[pl.BlockSpec((heads * A_QPAD, tq), lambda b, h, i: (h, b * nq + i)),
                  pl.BlockSpec((tq, heads * A_VDIM), lambda b, h, i: (b * nq + i, h)),
                  pl.BlockSpec((seq, heads * A_NOPE), lambda b, h, i: (b, h)),
                  pl.BlockSpec((seq, LANES), lambda b, h, i: (b, 0)),
                  pl.BlockSpec((heads * A_VDIM, seq), lambda b, h, i: (h, b))],
        out_specs=pl.BlockSpec((tq, heads * A_VDIM), lambda b, h, i: (b * nq + i, h)),
        out_shape=jax.ShapeDtypeStruct((t, A_HEADS * A_VDIM), BF16),
        scratch_shapes=[pltpu.VMEM((heads, tq, tq), F32), pltpu.VMEM((heads, tq, tq), F32),
                        pltpu.VMEM((heads, 1, tq), F32),
                        pltpu.VMEM((heads, A_VDIM + SUM_ROWS, tq), F32)],
        compiler_params=_params(("parallel", "parallel", "arbitrary")),
        name="mla_attention",
    )(qt, proj, kn, kr, vt)


def _swa_kernel(q_ref, z0_ref, z1_ref, z2_ref, kc_ref, kp_ref, vc_ref, vp_ref, bias_ref, sink_ref, o_ref,
                kd_ref, vf_ref, vt_ref, s_ref, p_ref, es_ref, *, tq):
    first = pl.program_id(2) == 0
    r = tq // BLOCK
    low = lax.broadcasted_iota(jnp.int32, (BLOCK, LANES), 1) < B_HEAD_DIM
    for lo, src in ((0, kp_ref), (BLOCK, kc_ref)):
        k = src[...].astype(F32)
        kr = pltpu.roll(k, B_HEAD_DIM, 1)
        half = lax.broadcasted_iota(jnp.int32, k.shape, 1) < B_HEAD_DIM
        kd_ref[0, lo:lo + k.shape[0], :] = jnp.where(half, k, kr).astype(BF16)
        kd_ref[1, lo:lo + k.shape[0], :] = jnp.where(half, kr, k).astype(BF16)
    vf_ref[0:BLOCK, :] = vp_ref[...]
    vf_ref[BLOCK:, :] = vc_ref[...]
    ones = jnp.ones((SUM_ROWS, 2 * BLOCK), BF16)
    for c in range(r):
        vt = vf_ref[c * BLOCK:(c + 2) * BLOCK, :].astype(F32).T
        for e in range(2):
            ve = vt[e * B_HEAD_DIM:(e + 1) * B_HEAD_DIM].astype(BF16)
            vt_ref[e, c] = jnp.concatenate([ve, ve, ones], axis=0)
    slot = lax.broadcasted_iota(jnp.int32, (BLOCK, 2 * BLOCK), 0)
    query = lax.broadcasted_iota(jnp.int32, (BLOCK, 2 * BLOCK), 1) & (BLOCK - 1)
    from_prev = slot > query
    pen0 = jnp.where(from_prev & first, NEG, 0.0).astype(F32)
    tiles = [(c, l) for c in range(r) for l in range(B_GROUP)]
    for t, (c, l) in enumerate(tiles):
        qp = q_ref[c * BLOCK:(c + 1) * BLOCK, l * LANES:(l + 1) * LANES]
        zero = jnp.zeros_like(qp)
        q2 = jnp.concatenate([jnp.where(low, qp, zero), jnp.where(low, zero, qp)], axis=0)
        sf = _dot_nt(kd_ref[l // 3, c * BLOCK:(c + 2) * BLOCK, :], q2)
        s = jnp.where(from_prev, sf[:BLOCK], sf[BLOCK:]) + bias_ref[0, l]
        s_ref[t] = s + pen0 if c == 0 else s
    for t, (c, l) in enumerate(tiles):
        s = s_ref[t]
        sink = sink_ref[0, l]
        m = jnp.maximum(jnp.max(s, axis=0, keepdims=True), sink)
        p = jnp.exp(s - m)
        es_ref[t] = jnp.exp(sink - m)
        zero = jnp.zeros_like(p)
        p_ref[t, 0:BLOCK] = jnp.where(from_prev, p, zero).astype(BF16)
        p_ref[t, BLOCK:] = jnp.where(from_prev, zero, p).astype(BF16)
    z_refs = (z0_ref, z1_ref, z2_ref)
    for t, (c, l) in enumerate(tiles):
        rows = slice(c * BLOCK, (c + 1) * BLOCK)
        ot = jnp.dot(vt_ref[l // 3, c], p_ref[t], preferred_element_type=F32)
        ot = ot[:LANES] * (1.0 / (ot[LANES:LANES + 1] + es_ref[t]))
        o = jnp.concatenate([ot[0:B_HEAD_DIM, 0:BLOCK], ot[B_HEAD_DIM:, BLOCK:]], axis=0).T
        z = z_refs[l // 2][rows, (l % 2) * LANES:(l % 2 + 1) * LANES].astype(F32)
        o_ref[rows, l * LANES:(l + 1) * LANES] = (o * _silu(z)).astype(o_ref.dtype)


def swa_attention(proj, bias, sink, batch, seq, tq=512):
    assert WINDOW == BLOCK
    t = proj.shape[0]
    tq = min(tq, seq)
    nq = seq // tq
    r = tq // BLOCK
    qw = B_GROUP * LANES
    k_off, v_off = 3072 // LANES, 3584 // LANES
    z_off = 5120 // (2 * LANES)

    def prev_map(p, b, i, off):
        return (jnp.maximum(b * (seq // BLOCK) + i * r - 1, b * (seq // BLOCK)), off + p)

    def z_map(p, b, i, part):
        return (b * nq + i, z_off + 3 * p + part)

    return pl.pallas_call(
        functools.partial(_swa_kernel, tq=tq),
        grid=(B_PAIRS, batch, nq),
        in_specs=[pl.BlockSpec((tq, qw), lambda p, b, i: (b * nq + i, p)),
                  pl.BlockSpec((tq, 2 * LANES), functools.partial(z_map, part=0)),
                  pl.BlockSpec((tq, 2 * LANES), functools.partial(z_map, part=1)),
                  pl.BlockSpec((tq, 2 * LANES), functools.partial(z_map, part=2)),
                  pl.BlockSpec((tq, LANES), lambda p, b, i: (b * nq + i, k_off + p)),
                  pl.BlockSpec((BLOCK, LANES), functools.partial(prev_map, off=k_off)),
                  pl.BlockSpec((tq, LANES), lambda p, b, i: (b * nq + i, v_off + p)),
                  pl.BlockSpec((BLOCK, LANES), functools.partial(prev_map, off=v_off)),
                  pl.BlockSpec((1, B_GROUP, BLOCK, 2 * BLOCK), lambda p, b, i: (p, 0, 0, 0)),
                  pl.BlockSpec((1, B_GROUP, 1, 2 * BLOCK), lambda p, b, i: (p, 0, 0, 0))],
        out_specs=pl.BlockSpec((tq, qw), lambda p, b, i: (b * nq + i, p)),
        out_shape=jax.ShapeDtypeStruct((t, SELF_WIDTH), BF16),
        scratch_shapes=[pltpu.VMEM((2, tq + BLOCK, LANES), BF16), pltpu.VMEM((tq + BLOCK, LANES), BF16),
                        pltpu.VMEM((2, r, LANES + SUM_ROWS, 2 * BLOCK), BF16),
                        pltpu.VMEM((r * B_GROUP, BLOCK, 2 * BLOCK), F32),
                        pltpu.VMEM((r * B_GROUP, 2 * BLOCK, 2 * BLOCK), BF16),
                        pltpu.VMEM((r * B_GROUP, 1, 2 * BLOCK), F32)],
        compiler_params=_params(("parallel", "parallel", "arbitrary")),
        name="swa_attention",
    )(proj, proj, proj, proj, proj, proj, proj, proj, bias, sink)


def _memattn_kernel(xq_ref, z_ref, mk_ref, mv_ref, o_ref):
    for h in range(X_HEADS):
        cols = slice(h * X_HEAD_DIM, (h + 1) * X_HEAD_DIM)
        s = _dot_nt(xq_ref[:, cols], mk_ref[:, cols])
        m = jnp.max(s, axis=1, keepdims=True)
        p = jnp.exp(s - m)
        l = jnp.sum(p, axis=1, keepdims=True)
        o = jnp.dot(p.astype(BF16), mv_ref[:, cols], preferred_element_type=F32) / l
        o_ref[:, cols] = (o * _silu(z_ref[:, cols].astype(F32))).astype(o_ref.dtype)


def mem_attention(proj, mkv, layer, batch, seq, xq_blk, z_blk, tq=512):
    t = proj.shape[0]
    n_mem = mkv.shape[1] // batch
    tq = min(tq, seq)
    nq = seq // tq
    return pl.pallas_call(
        _memattn_kernel,
        grid=(batch, nq),
        in_specs=[pl.BlockSpec((tq, X_WIDTH), lambda b, i: (b * nq + i, xq_blk)),
                  pl.BlockSpec((tq, X_WIDTH), lambda b, i: (b * nq + i, z_blk)),
                  pl.BlockSpec((None, n_mem, X_WIDTH), lambda b, i: (layer, b, 0)),
                  pl.BlockSpec((None, n_mem, X_WIDTH), lambda b, i: (layer, b, 1))],
        out_specs=pl.BlockSpec((tq, X_WIDTH), lambda b, i: (b * nq + i, 0)),
        out_shape=jax.ShapeDtypeStruct((t, X_WIDTH), BF16),
        compiler_params=_params(("parallel", "arbitrary")),
        name="mem_attention",
    )(proj, proj, mkv, mkv)


def _outproj_kernel(ys_ref, ym_ref, x_ref, ws_ref, wm_ref, o_ref, ob_ref, ss_ref):
    acc = jnp.dot(ys_ref[...], ws_ref[...], preferred_element_type=F32)
    acc = acc + jnp.dot(ym_ref[...], wm_ref[...], preferred_element_type=F32)
    x = x_ref[...] + acc
    o_ref[...] = x
    ob_ref[...] = x.astype(BF16)
    part = jnp.sum(x * x, axis=-1, keepdims=True)

    @pl.when(pl.program_id(1) == 0)
    def _():
        ss_ref[...] = part

    @pl.when(pl.program_id(1) != 0)
    def _():
        ss_ref[...] += part


def out_proj(y_self, y_mem, x, w_out, layer, tm=1024, tn=512):
    t, d = x.shape
    tm = min(tm, t)
    return pl.pallas_call(
        _outproj_kernel,
        grid=(t // tm, d // tn),
        in_specs=[pl.BlockSpec((tm, SELF_WIDTH), lambda i, j: (i, 0)),
                  pl.BlockSpec((tm, X_WIDTH), lambda i, j: (i, 0)),
                  pl.BlockSpec((tm, tn), lambda i, j: (i, j)),
                  pl.BlockSpec((None, SELF_WIDTH, tn), lambda i, j: (layer, 0, j)),
                  pl.BlockSpec((None, X_WIDTH, tn), lambda i, j: (layer, SELF_WIDTH // X_WIDTH, j))],
        out_specs=[pl.BlockSpec((tm, tn), lambda i, j: (i, j)),
                   pl.BlockSpec((tm, tn), lambda i, j: (i, j)),
                   pl.BlockSpec((tm, 1), lambda i, j: (i, 0))],
        out_shape=[jax.ShapeDtypeStruct((t, d), F32), jax.ShapeDtypeStruct((t, d), BF16),
                   jax.ShapeDtypeStruct((t, 1), F32)],
        compiler_params=_params(("parallel", "arbitrary")),
        name="out_proj",
    )(y_self, y_mem, x, w_out, w_out)


A_IN_PAD = 6912


def _relayout_a_in_kernel(wt_ref, g_ref, o_ref):
    o1, o2, o3 = A_Q_RANK, A_Q_RANK + A_KV_RANK, A_Q_RANK + A_KV_RANK + A_ROPE
    o4 = o3 + X_WIDTH
    g = g_ref[0]
    chunk = 512

    def put(dst, src, n, scale):
        for c in range(0, n, chunk):
            x = wt_ref[0, src + c:src + c + chunk, :] * (g * scale)
            o_ref[0, :, dst + c:dst + c + chunk] = x.T.astype(BF16)

    put(0, o4, 4096, 1.0)
    put(4096, 0, A_Q_RANK, 1.0)
    put(5120, o3, X_WIDTH, X_HEAD_DIM ** -0.5)
    put(6144, o1, A_KV_RANK, 1.0)
    half = A_ROPE // 2
    t1 = wt_ref[0, o2:o2 + half, :]
    t2 = wt_ref[0, o2 + half:o3, :]
    kr = jnp.concatenate([t1, t2, t2, t1], axis=0) * g
    o_ref[0, :, 6656:6784] = kr.T.astype(BF16)
    o_ref[0, :, 6784:] = jnp.zeros((o_ref.shape[1], A_IN_PAD - 6784), BF16)


def relayout_a_in(w, g, tk=256):
    n_layers, k, n = w.shape
    return pl.pallas_call(
        _relayout_a_in_kernel,
        grid=(n_layers, k // tk),
        in_specs=[pl.BlockSpec((1, n, tk), lambda a, i: (a, 0, i)),
                  pl.BlockSpec((1, 1, tk), lambda a, i: (a, 0, i))],
        out_specs=pl.BlockSpec((1, tk, A_IN_PAD), lambda a, i: (a, i, 0)),
        out_shape=jax.ShapeDtypeStruct((n_layers, k, A_IN_PAD), BF16),
        compiler_params=_params(("parallel", "parallel")),
        name="relayout_a_w_in",
    )(jnp.swapaxes(w, 1, 2), g.astype(F32)[:, None, :])


def _rope_tables(positions):
    inv = ROPE_THETA ** (-jnp.arange(0, A_ROPE, 2, dtype=F32) / A_ROPE)
    ang = positions.astype(F32).reshape(-1)[:, None] * inv
    cos, sin = jnp.cos(ang), jnp.sin(ang)
    zero = jnp.zeros_like(cos)
    ct = jnp.concatenate([cos, cos, zero, zero], axis=1)
    st = jnp.concatenate([-sin, sin, zero, zero], axis=1)
    return ct, st


def _swap_halves(w):
    half = w.shape[-1] // 2
    return jnp.concatenate([w[..., half:], w[..., :half]], axis=-1)


def _prep_a_w_qb(w):
    n_layers = w.shape[0]
    w = w.astype(BF16).reshape(n_layers, A_Q_RANK, A_HEADS, A_NOPE + A_ROPE)
    rope = w[..., A_NOPE:]
    out = jnp.concatenate([w[..., :A_NOPE], rope, _swap_halves(rope)], axis=-1)
    return out.reshape(n_layers, A_Q_RANK, A_HEADS * A_QPAD).swapaxes(1, 2)


def _prep_a_w_kvb(w):
    n_layers = w.shape[0]
    w = w.astype(BF16).reshape(n_layers, A_KV_RANK, A_HEADS, A_NOPE + A_VDIM)
    wk = w[..., :A_NOPE].reshape(n_layers, A_KV_RANK, A_HEADS * A_NOPE)
    wvt = w[..., A_NOPE:].reshape(n_layers, A_KV_RANK, A_HEADS * A_VDIM).swapaxes(1, 2)
    return wk, wvt


def _prep_b_w_in(w, g):
    nq = B_HEADS * B_HEAD_DIM
    nk = B_KV_HEADS * B_HEAD_DIM
    col = jnp.arange(w.shape[-1])
    scale = jnp.where(col < nq, B_HEAD_DIM ** -0.5,
                      jnp.where((col >= nq + 2 * nk) & (col < nq + 2 * nk + X_WIDTH), X_HEAD_DIM ** -0.5, 1.0))
    return (w * (g.astype(F32)[:, :, None] * scale.astype(F32))).astype(BF16)


def _t5_bucket(dist):
    n = jnp.maximum(dist, 0)
    nf = jnp.maximum(n, 1).astype(F32)
    large = MAX_EXACT + (jnp.log(nf / MAX_EXACT) / math.log(MAX_DIST / MAX_EXACT)
                         * (N_BUCKETS - MAX_EXACT)).astype(jnp.int32)
    large = jnp.minimum(large, N_BUCKETS - 1)
    return jnp.where(n < MAX_EXACT, n, large)


def _swa_tables(rel_bias, sinks):
    q_local = jnp.arange(BLOCK)[None, :]
    slot = jnp.arange(BLOCK)[:, None]
    dist = jnp.where(slot > q_local, q_local + BLOCK - slot, q_local - slot)
    onehot = (_t5_bucket(dist)[:, :, None] == jnp.arange(N_BUCKETS)).astype(F32)
    bias = jnp.einsum("kqn,nh->kqh", onehot, rel_bias.astype(F32), precision=lax.Precision.HIGHEST)
    bias = bias.reshape(BLOCK, BLOCK, B_PAIRS, B_GROUP, 2)
    bias = bias.transpose(2, 3, 0, 4, 1).reshape(B_PAIRS, B_GROUP, BLOCK, 2 * BLOCK)
    sink = sinks.astype(F32).reshape(B_PAIRS, B_GROUP, 2, 1)
    sink = jnp.broadcast_to(sink, (B_PAIRS, B_GROUP, 2, BLOCK)).reshape(B_PAIRS, B_GROUP, 1, 2 * BLOCK)
    return bias, sink


def kernel(x, mem, positions, norm_g, mem_norm_g, final_norm_g, w_mem_kv, w_out, a_w_in, a_q_norm_g, a_kv_norm_g,
           a_w_qb, a_w_kvb, b_w_in, b_sinks, rel_bias):
    batch, seq, d = x.shape
    n_mem = mem.shape[1]
    depth = norm_g.shape[0]
    xs = x.reshape(batch * seq, d)
    mems = mem.reshape(batch * n_mem, d)
    ct, st = _rope_tables(positions)
    ctt, stt = ct.T, st.T
    mkv = mem_kv_all(mems, mem_norm_g, w_mem_kv.astype(BF16))
    w_out_b = w_out.astype(BF16)
    a_w_in_b = relayout_a_in(a_w_in, norm_g[0::2])
    b_w_in_b = _prep_b_w_in(b_w_in, norm_g[1::2])
    a_w_qb_b = _prep_a_w_qb(a_w_qb)
    a_wk_b, a_wvt_b = _prep_a_w_kvb(a_w_kvb)

    xb, ss = cast_sumsq(xs)
    for i in range(depth):
        j = i // 2
        if i % 2 == 0:
            proj = norm_matmul(xb, ss, a_w_in_b, j, "mla_in_proj", tn=768)
            qt = mla_q(proj, a_q_norm_g[j], a_w_qb_b, j, ctt, stt, cq_blk=4)
            kn, vt, kr = mla_kv(proj, a_kv_norm_g[j], a_wk_b, a_wvt_b, j, ct, st, ckv_blk=12, kr_blk=52)
            y_self = mla_attention(qt, proj, kn, vt, kr, batch, seq)
            y_mem = mem_attention(proj, mkv, i, batch, seq, xq_blk=5, z_blk=3)
        else:
            proj = norm_matmul(xb, ss, b_w_in_b, j, "swa_in_proj")
            bias, sink = _swa_tables(rel_bias, b_sinks[j])
            y_self = swa_attention(proj, bias, sink, batch, seq)
            y_mem = mem_attention(proj, mkv, i, batch, seq, xq_blk=4, z_blk=8)
        xs, xb, ss = out_proj(y_self, y_mem, xs, w_out_b, i)
    out = rmsnorm_from_sumsq(xs, ss, final_norm_g)
    return out.reshape(batch, seq, d)
```

```python
import functools
import math

import jax
import jax.numpy as jnp
from jax import lax
from jax.experimental import pallas as pl
from jax.experimental.pallas import tpu as pltpu

F32 = jnp.float32
BF16 = jnp.bfloat16

EPS = 1e-6
LANES = 128
NEG = -1e30
VMEM_LIMIT = 56 * 1024 * 1024

X_HEADS = 4
X_HEAD_DIM = 256
X_WIDTH = 1024
SELF_WIDTH = 3072
A_NOPE = 128
A_ROPE = 64
A_VDIM = 128
A_HEADS = 24
A_Q_RANK = 1024
A_KV_RANK = 512
SUM_ROWS = 16
A_QPAD = 256
ROPE_THETA = 10000.0
B_HEAD_DIM = 64
B_HEADS = 48
B_KV_HEADS = 8
B_GROUP = 6
B_PAIRS = B_KV_HEADS // 2
WINDOW = 128
BLOCK = 128
N_BUCKETS = 32
MAX_EXACT = 16
MAX_DIST = 128


def _params(sem):
    return pltpu.CompilerParams(dimension_semantics=sem, vmem_limit_bytes=VMEM_LIMIT)


def _rms(x, g):
    ms = jnp.mean(x * x, axis=-1, keepdims=True)
    return x * lax.rsqrt(ms + EPS) * g


def _silu(z):
    return z * jax.nn.sigmoid(z)


def _dot_nt(a, b):
    return lax.dot_general(a, b, (((1,), (1,)), ((), ())), preferred_element_type=F32)


def _mem_kv_kernel(mem_ref, g_ref, w_ref, o_ref, mn_ref):
    @pl.when(pl.program_id(2) == 0)
    def _():
        mn_ref[...] = _rms(mem_ref[...], g_ref[...]).astype(BF16)

    o_ref[...] = jnp.dot(mn_ref[...], w_ref[...], preferred_element_type=F32).astype(o_ref.dtype)


def mem_kv_all(mem, g, w, tm=512, tn=1024):
    m, d = mem.shape
    n_layers, _, n = w.shape
    tm = min(tm, m)
    return pl.pallas_call(
        _mem_kv_kernel,
        grid=(n_layers, m // tm, n // tn),
        in_specs=[pl.BlockSpec((tm, d), lambda l, i, j: (i, 0)),
                  pl.BlockSpec((None, 1, d), lambda l, i, j: (l, 0, 0)),
                  pl.BlockSpec((None, d, tn), lambda l, i, j: (l, 0, j))],
        out_specs=pl.BlockSpec((None, tm, tn), lambda l, i, j: (l, i, j)),
        out_shape=jax.ShapeDtypeStruct((n_layers, m, n), BF16),
        scratch_shapes=[pltpu.VMEM((tm, d), BF16)],
        compiler_params=_params(("parallel", "parallel", "arbitrary")),
        name="mem_kv_proj",
    )(mem, g.astype(F32)[:, None, :], w)


def _cast_sumsq_kernel(x_ref, xb_ref, ss_ref):
    x = x_ref[...]
    xb_ref[...] = x.astype(BF16)
    ss_ref[...] = jnp.sum(x * x, axis=-1, keepdims=True)


def cast_sumsq(x, tm=256):
    m, d = x.shape
    tm = min(tm, m)
    return pl.pallas_call(
        _cast_sumsq_kernel,
        grid=(m // tm,),
        in_specs=[pl.BlockSpec((tm, d), lambda i: (i, 0))],
        out_specs=[pl.BlockSpec((tm, d), lambda i: (i, 0)), pl.BlockSpec((tm, 1), lambda i: (i, 0))],
        out_shape=[jax.ShapeDtypeStruct((m, d), BF16), jax.ShapeDtypeStruct((m, 1), F32)],
        compiler_params=_params(("parallel",)),
        name="cast_sumsq",
    )(x)


def _norm_matmul_kernel(a_ref, ss_ref, b_ref, o_ref, *, width):
    r = lax.rsqrt(ss_ref[...] * (1.0 / width) + EPS)
    o_ref[...] = (jnp.dot(a_ref[...], b_ref[...], preferred_element_type=F32) * r).astype(o_ref.dtype)


def norm_matmul(a, ss, b, layer, name, tm=1024, tn=1024):
    m, k = a.shape
    n = b.shape[2]
    tm, tn = min(tm, m), min(tn, n)
    return pl.pallas_call(
        functools.partial(_norm_matmul_kernel, width=k),
        grid=(m // tm, n // tn),
        in_specs=[pl.BlockSpec((tm, k), lambda i, j: (i, 0)),
                  pl.BlockSpec((tm, 1), lambda i, j: (i, 0)),
                  pl.BlockSpec((None, k, tn), lambda i, j: (layer, 0, j))],
        out_specs=pl.BlockSpec((tm, tn), lambda i, j: (i, j)),
        out_shape=jax.ShapeDtypeStruct((m, n), BF16),
        compiler_params=_params(("parallel", "arbitrary")),
        name=name,
    )(a, ss, b)


def _scale_rows_kernel(x_ref, ss_ref, g_ref, o_ref):
    x = x_ref[...]
    r = lax.rsqrt(ss_ref[...] * (1.0 / x.shape[-1]) + EPS)
    o_ref[...] = (x * r * g_ref[...]).astype(o_ref.dtype)


def rmsnorm_from_sumsq(x, ss, g, tm=256):
    m, d = x.shape
    tm = min(tm, m)
    return pl.pallas_call(
        _scale_rows_kernel,
        grid=(m // tm,),
        in_specs=[pl.BlockSpec((tm, d), lambda i: (i, 0)),
                  pl.BlockSpec((tm, 1), lambda i: (i, 0)),
                  pl.BlockSpec((1, d), lambda i: (0, 0))],
        out_specs=pl.BlockSpec((tm, d), lambda i: (i, 0)),
        out_shape=jax.ShapeDtypeStruct((m, d), x.dtype),
        compiler_params=_params(("parallel",)),
        name="final_rmsnorm",
    )(x, ss, g.reshape(1, d).astype(F32))


def _rope(u, ct, st):
    return u * ct + pltpu.roll(u, 64, 1) * st


def _qb_kernel(cq_ref, g_ref, wt_ref, ctt_ref, stt_ref, o_ref, cqn_ref, *, heads, scale):
    @pl.when(pl.program_id(1) == 0)
    def _():
        cqn_ref[...] = _rms(cq_ref[...].astype(F32), g_ref[...]).astype(BF16)

    cqn = cqn_ref[...]
    ctt = ctt_ref[...] * scale
    stt = stt_ref[...] * scale
    for h in range(heads):
        lo = h * A_QPAD
        acc = _dot_nt(wt_ref[lo:lo + A_QPAD, :], cqn)
        o_ref[lo:lo + A_NOPE, :] = (acc[:A_NOPE] * scale).astype(BF16)
        u = acc[A_NOPE:]
        swapped = jnp.concatenate([u[2 * A_ROPE // 2:], u[:2 * A_ROPE // 2]], axis=0)
        o_ref[lo + A_NOPE:lo + A_QPAD, :] = (u * ctt + swapped * stt).astype(BF16)


def mla_q(proj, g, w_qbt, layer, ctt, stt, cq_blk, tm=1024, heads=6):
    t = proj.shape[0]
    tm = min(tm, t)
    n = A_HEADS * A_QPAD
    tn = heads * A_QPAD
    scale = (A_NOPE + A_ROPE) ** -0.5 * math.log2(math.e)
    return pl.pallas_call(
        functools.partial(_qb_kernel, heads=heads, scale=scale),
        grid=(t // tm, n // tn),
        in_specs=[pl.BlockSpec((tm, A_Q_RANK), lambda i, j: (i, cq_blk)),
                  pl.BlockSpec((1, A_Q_RANK), lambda i, j: (0, 0)),
                  pl.BlockSpec((None, tn, A_Q_RANK), lambda i, j: (layer, j, 0)),
                  pl.BlockSpec((LANES, tm), lambda i, j: (0, i)),
                  pl.BlockSpec((LANES, tm), lambda i, j: (0, i))],
        out_specs=pl.BlockSpec((tn, tm), lambda i, j: (j, i)),
        out_shape=jax.ShapeDtypeStruct((n, t), BF16),
        scratch_shapes=[pltpu.VMEM((tm, A_Q_RANK), BF16)],
        compiler_params=_params(("parallel", "arbitrary")),
        name="mla_q_up",
    )(proj, g.reshape(1, -1).astype(F32), w_qbt, ctt, stt)


def _kvb_kernel(ckv_ref, g_ref, wk_ref, wvt_ref, kr_ref, ct_ref, st_ref, kn_ref, vt_ref, kro_ref, cn_ref):
    @pl.when(pl.program_id(1) == 0)
    def _():
        cn_ref[...] = _rms(ckv_ref[...].astype(F32), g_ref[...]).astype(BF16)
        kro_ref[...] = _rope(kr_ref[...].astype(F32), ct_ref[...], st_ref[...]).astype(BF16)

    cn = cn_ref[...]
    kn_ref[...] = jnp.dot(cn, wk_ref[...], preferred_element_type=F32).astype(BF16)
    vt_ref[...] = _dot_nt(wvt_ref[...], cn).astype(BF16)


def mla_kv(proj, g, wk, wvt, layer, ct, st, ckv_blk, kr_blk, tm=1024, tn=1536):
    t = proj.shape[0]
    tm = min(tm, t)
    n = wk.shape[2]
    return pl.pallas_call(
        _kvb_kernel,
        grid=(t // tm, n // tn),
        in_specs=[pl.BlockSpec((tm, A_KV_RANK), lambda i, j: (i, ckv_blk)),
                  pl.BlockSpec((1, A_KV_RANK), lambda i, j: (0, 0)),
                  pl.BlockSpec((None, A_KV_RANK, tn), lambda i, j: (layer, 0, j)),
                  pl.BlockSpec((None, tn, A_KV_RANK), lambda i, j: (layer, j, 0)),
                  pl.BlockSpec((tm, LANES), lambda i, j: (i, kr_blk)),
                  pl.BlockSpec((tm, LANES), lambda i, j: (i, 0)),
                  pl.BlockSpec((tm, LANES), lambda i, j: (i, 0))],
        out_specs=[pl.BlockSpec((tm, tn), lambda i, j: (i, j)),
                   pl.BlockSpec((tn, tm), lambda i, j: (j, i)),
                   pl.BlockSpec((tm, LANES), lambda i, j: (i, 0))],
        out_shape=[jax.ShapeDtypeStruct((t, n), BF16),
                   jax.ShapeDtypeStruct((n, t), BF16),
                   jax.ShapeDtypeStruct((t, LANES), BF16)],
        scratch_shapes=[pltpu.VMEM((tm, A_KV_RANK), BF16)],
        compiler_params=_params(("parallel", "arbitrary")),
        name="mla_kv_up",
    )(proj, g.reshape(1, -1).astype(F32), wk, wvt, proj, ct, st)


def _mla_attn_kernel(qt_ref, z_ref, kn_ref, kr_ref, vt_ref, o_ref, sa_ref, sb_ref, m_ref, acc_ref, *,
                     tq, heads):
    qi = pl.program_id(2)
    m_ref[...] = jnp.full(m_ref.shape, NEG, F32)
    acc_ref[...] = jnp.zeros(acc_ref.shape, F32)
    ones = jnp.ones((SUM_ROWS, tq), BF16)

    def scores(j, s_ref):
        rows = pl.ds(pl.multiple_of(j * tq, tq), tq)
        kr = kr_ref[rows, :]
        for h in range(heads):
            k = jnp.concatenate([kn_ref[rows, h * A_NOPE:(h + 1) * A_NOPE], kr], axis=1)
            s_ref[h] = jnp.dot(k, qt_ref[h * A_QPAD:(h + 1) * A_QPAD, :], preferred_element_type=F32)

    def update(j, s_ref, masked):
        rows = pl.ds(pl.multiple_of(j * tq, tq), tq)
        for h in range(heads):
            s = s_ref[h]
            if masked:
                r = lax.broadcasted_iota(jnp.int32, s.shape, 0)
                c = lax.broadcasted_iota(jnp.int32, s.shape, 1)
                s = jnp.where(r <= c, s, NEG)
            m_prev = m_ref[h]
            m_cur = jnp.maximum(m_prev, jnp.max(s, axis=0, keepdims=True))
            alpha = jnp.exp2(m_prev - m_cur)
            p = jnp.exp2(s - m_cur).astype(BF16)
            va = jnp.concatenate([vt_ref[h * A_VDIM:(h + 1) * A_VDIM, rows], ones], axis=0)
            acc_ref[h] = alpha * acc_ref[h] + jnp.dot(va, p, preferred_element_type=F32)
            m_ref[h] = m_cur

    even = (qi % 2) == 1

    @pl.when(even)
    def _():
        scores(0, sb_ref)
        scores(1, sa_ref)
        update(0, sb_ref, False)

    @pl.when(jnp.logical_not(even))
    def _():
        scores(0, sa_ref)

    first = even.astype(jnp.int32)

    def pair(t, carry):
        c = first + 2 * t
        scores(c + 1, sb_ref)
        update(c, sa_ref, False)
        scores(c + 2, sa_ref)
        update(c + 1, sb_ref, False)
        return carry

    lax.fori_loop(0, (qi - first) // 2, pair, 0)
    update(qi, sa_ref, True)
    for h in range(heads):
        acc = acc_ref[h]
        o = (acc[:A_VDIM] * (1.0 / acc[A_VDIM:A_VDIM + 1])).T
        gate = _silu(z_ref[:, h * A_VDIM:(h + 1) * A_VDIM].astype(F32))
        o_ref[:, h * A_VDIM:(h + 1) * A_VDIM] = (o * gate).astype(o_ref.dtype)


def mla_attention(qt, proj, kn, vt, kr, batch, seq, tq=512, heads=6):
    t = proj.shape[0]
    tq = min(tq, seq)
    nq = seq // tq
    return pl.pallas_call(
        functools.partial(_mla_attn_kernel, tq=tq, heads=heads),
        grid=(batch, A_HEADS // heads, nq),
        in_specs=[pl.BlockSpec((heads * A_QPAD, tq), lambda b, h, i: (h, b * nq + i)),
                  pl.BlockSpec((tq, heads * A_VDIM), lambda b, h, i: (b * nq + i, h)),
                  pl.BlockSpec((seq, heads * A_NOPE), lambda b, h, i: (b, h)),
                  pl.BlockSpec((seq, LANES), lambda b, h, i: (b, 0)),
                  pl.BlockSpec((heads * A_VDIM, seq), lambda b, h, i: (h, b))],
        out_specs=pl.BlockSpec((tq, heads * A_VDIM), lambda b, h, i: (b * nq + i, h)),
        out_shape=jax.ShapeDtypeStruct((t, A_HEADS * A_VDIM), BF16),
        scratch_shapes=[pltpu.VMEM((heads, tq, tq), F32), pltpu.VMEM((heads, tq, tq), F32),
                        pltpu.VMEM((heads, 1, tq), F32),
                        pltpu.VMEM((heads, A_VDIM + SUM_ROWS, tq), F32)],
        compiler_params=_params(("parallel", "parallel", "arbitrary")),
        name="mla_attention",
    )(qt, proj, kn, kr, vt)


def _swa_kernel(q_ref, z0_ref, z1_ref, z2_ref, kc_ref, kp_ref, vc_ref, vp_ref, bias_ref, sink_ref, o_ref,
                kd_ref, vf_ref, vt_ref, s_ref, p_ref, es_ref, *, tq):
    first = pl.program_id(2) == 0
    r = tq // BLOCK
    low = lax.broadcasted_iota(jnp.int32, (BLOCK, LANES), 1) < B_HEAD_DIM
    for lo, src in ((0, kp_ref), (BLOCK, kc_ref)):
        k = src[...].astype(F32)
        kr = pltpu.roll(k, B_HEAD_DIM, 1)
        half = lax.broadcasted_iota(jnp.int32, k.shape, 1) < B_HEAD_DIM
        kd_ref[0, lo:lo + k.shape[0], :] = jnp.where(half, k, kr).astype(BF16)
        kd_ref[1, lo:lo + k.shape[0], :] = jnp.where(half, kr, k).astype(BF16)
    vf_ref[0:BLOCK, :] = vp_ref[...]
    vf_ref[BLOCK:, :] = vc_ref[...]
    ones = jnp.ones((SUM_ROWS, 2 * BLOCK), BF16)
    for c in range(r):
        vt = vf_ref[c * BLOCK:(c + 2) * BLOCK, :].astype(F32).T
        for e in range(2):
            ve = vt[e * B_HEAD_DIM:(e + 1) * B_HEAD_DIM].astype(BF16)
            vt_ref[e, c] = jnp.concatenate([ve, ve, ones], axis=0)
    slot = lax.broadcasted_iota(jnp.int32, (BLOCK, 2 * BLOCK), 0)
    query = lax.broadcasted_iota(jnp.int32, (BLOCK, 2 * BLOCK), 1) & (BLOCK - 1)
    from_prev = slot > query
    pen0 = jnp.where(from_prev & first, NEG, 0.0).astype(F32)
    tiles = [(c, l) for c in range(r) for l in range(B_GROUP)]
    for t, (c, l) in enumerate(tiles):
        qp = q_ref[c * BLOCK:(c + 1) * BLOCK, l * LANES:(l + 1) * LANES]
        zero = jnp.zeros_like(qp)
        q2 = jnp.concatenate([jnp.where(low, qp, zero), jnp.where(low, zero, qp)], axis=0)
        sf = _dot_nt(kd_ref[l // 3, c * BLOCK:(c + 2) * BLOCK, :], q2)
        s = jnp.where(from_prev, sf[:BLOCK], sf[BLOCK:]) + bias_ref[0, l]
        s_ref[t] = s + pen0 if c == 0 else s
    for t, (c, l) in enumerate(tiles):
        s = s_ref[t]
        sink = sink_ref[0, l]
        m = jnp.maximum(jnp.max(s, axis=0, keepdims=True), sink)
        p = jnp.exp(s - m)
        es_ref[t] = jnp.exp(sink - m)
        zero = jnp.zeros_like(p)
        p_ref[t, 0:BLOCK] = jnp.where(from_prev, p, zero).astype(BF16)
        p_ref[t, BLOCK:] = jnp.where(from_prev, zero, p).astype(BF16)
    z_refs = (z0_ref, z1_ref, z2_ref)
    for t, (c, l) in enumerate(tiles):
        rows = slice(c * BLOCK, (c + 1) * BLOCK)
        ot = jnp.dot(vt_ref[l // 3, c], p_ref[t], preferred_element_type=F32)
        ot = ot[:LANES] * (1.0 / (ot[LANES:LANES + 1] + es_ref[t]))
        o = jnp.concatenate([ot[0:B_HEAD_DIM, 0:BLOCK], ot[B_HEAD_DIM:, BLOCK:]], axis=0).T
        z = z_refs[l // 2][rows, (l % 2) * LANES:(l % 2 + 1) * LANES].astype(F32)
        o_ref[rows, l * LANES:(l + 1) * LANES] = (o * _silu(z)).astype(o_ref.dtype)


def swa_attention(proj, bias, sink, batch, seq, tq=1024):
    assert WINDOW == BLOCK
    t = proj.shape[0]
    tq = min(tq, seq)
    nq = seq // tq
    r = tq // BLOCK
    qw = B_GROUP * LANES
    k_off, v_off = 3072 // LANES, 3584 // LANES
    z_off = 5120 // (2 * LANES)

    def prev_map(p, b, i, off):
        return (jnp.maximum(b * (seq // BLOCK) + i * r - 1, b * (seq // BLOCK)), off + p)

    def z_map(p, b, i, part):
        return (b * nq + i, z_off + 3 * p + part)

    return pl.pallas_call(
        functools.partial(_swa_kernel, tq=tq),
        grid=(B_PAIRS, batch, nq),
        in_specs=[pl.BlockSpec((tq, qw), lambda p, b, i: (b * nq + i, p)),
                  pl.BlockSpec((tq, 2 * LANES), functools.partial(z_map, part=0)),
                  pl.BlockSpec((tq, 2 * LANES), functools.partial(z_map, part=1)),
                  pl.BlockSpec((tq, 2 * LANES), functools.partial(z_map, part=2)),
                  pl.BlockSpec((tq, LANES), lambda p, b, i: (b * nq + i, k_off + p)),
                  pl.BlockSpec((BLOCK, LANES), functools.partial(prev_map, off=k_off)),
                  pl.BlockSpec((tq, LANES), lambda p, b, i: (b * nq + i, v_off + p)),
                  pl.BlockSpec((BLOCK, LANES), functools.partial(prev_map, off=v_off)),
                  pl.BlockSpec((1, B_GROUP, BLOCK, 2 * BLOCK), lambda p, b, i: (p, 0, 0, 0)),
                  pl.BlockSpec((1, B_GROUP, 1, 2 * BLOCK), lambda p, b, i: (p, 0, 0, 0))],
        out_specs=pl.BlockSpec((tq, qw), lambda p, b, i: (b * nq + i, p)),
        out_shape=jax.ShapeDtypeStruct((t, SELF_WIDTH), BF16),
        scratch_shapes=[pltpu.VMEM((2, tq + BLOCK, LANES), BF16), pltpu.VMEM((tq + BLOCK, LANES), BF16),
                        pltpu.VMEM((2, r, LANES + SUM_ROWS, 2 * BLOCK), BF16),
                        pltpu.VMEM((r * B_GROUP, BLOCK, 2 * BLOCK), F32),
                        pltpu.VMEM((r * B_GROUP, 2 * BLOCK, 2 * BLOCK), BF16),
                        pltpu.VMEM((r * B_GROUP, 1, 2 * BLOCK), F32)],
        compiler_params=_params(("parallel", "parallel", "arbitrary")),
        name="swa_attention",
    )(proj, proj, proj, proj, proj, proj, proj, proj, bias, sink)


def _memattn_kernel(xq_ref, z_ref, mk_ref, mv_ref, o_ref):
    for h in range(X_HEADS):
        cols = slice(h * X_HEAD_DIM, (h + 1) * X_HEAD_DIM)
        s = _dot_nt(xq_ref[:, cols], mk_ref[:, cols])
        m = jnp.max(s, axis=1, keepdims=True)
        p = jnp.exp(s - m)
        l = jnp.sum(p, axis=1, keepdims=True)
        o = jnp.dot(p.astype(BF16), mv_ref[:, cols], preferred_element_type=F32) / l
        o_ref[:, cols] = (o * _silu(z_ref[:, cols].astype(F32))).astype(o_ref.dtype)


def mem_attention(proj, mkv, layer, batch, seq, xq_blk, z_blk, tq=512):
    t = proj.shape[0]
    n_mem = mkv.shape[1] // batch
    tq = min(tq, seq)
    nq = seq // tq
    return pl.pallas_call(
        _memattn_kernel,
        grid=(batch, nq),
        in_specs=[pl.BlockSpec((tq, X_WIDTH), lambda b, i: (b * nq + i, xq_blk)),
                  pl.BlockSpec((tq, X_WIDTH), lambda b, i: (b * nq + i, z_blk)),
                  pl.BlockSpec((None, n_mem, X_WIDTH), lambda b, i: (layer, b, 0)),
                  pl.BlockSpec((None, n_mem, X_WIDTH), lambda b, i: (layer, b, 1))],
        out_specs=pl.BlockSpec((tq, X_WIDTH), lambda b, i: (b * nq + i, 0)),
        out_shape=jax.ShapeDtypeStruct((t, X_WIDTH), BF16),
        compiler_params=_params(("parallel", "arbitrary")),
        name="mem_attention",
    )(proj, proj, mkv, mkv)


def _outproj_kernel(ys_ref, ym_ref, x_ref, ws_ref, wm_ref, o_ref, ob_ref, ss_ref):
    acc = jnp.dot(ys_ref[...], ws_ref[...], preferred_element_type=F32)
    acc = acc + jnp.dot(ym_ref[...], wm_ref[...], preferred_element_type=F32)
    x = x_ref[...] + acc
    o_ref[...] = x
    ob_ref[...] = x.astype(BF16)
    part = jnp.sum(x * x, axis=-1, keepdims=True)

    @pl.when(pl.program_id(1) == 0)
    def _():
        ss_ref[...] = part

    @pl.when(pl.program_id(1) != 0)
    def _():
        ss_ref[...] += part


def out_proj(y_self, y_mem, x, w_out, layer, tm=1024, tn=512):
    t, d = x.shape
    tm = min(tm, t)
    return pl.pallas_call(
        _outproj_kernel,
        grid=(t // tm, d // tn),
        in_specs=[pl.BlockSpec((tm, SELF_WIDTH), lambda i, j: (i, 0)),
                  pl.BlockSpec((tm, X_WIDTH), lambda i, j: (i, 0)),
                  pl.BlockSpec((tm, tn), lambda i, j: (i, j)),
                  pl.BlockSpec((None, SELF_WIDTH, tn), lambda i, j: (layer, 0, j)),
                  pl.BlockSpec((None, X_WIDTH, tn), lambda i, j: (layer, SELF_WIDTH // X_WIDTH, j))],
        out_specs=[pl.BlockSpec((tm, tn), lambda i, j: (i, j)),
                   pl.BlockSpec((tm, tn), lambda i, j: (i, j)),
                   pl.BlockSpec((tm, 1), lambda i, j: (i, 0))],
        out_shape=[jax.ShapeDtypeStruct((t, d), F32), jax.ShapeDtypeStruct((t, d), BF16),
                   jax.ShapeDtypeStruct((t, 1), F32)],
        compiler_params=_params(("parallel", "arbitrary")),
        name="out_proj",
    )(y_self, y_mem, x, w_out, w_out)


A_IN_PAD = 6912


def _relayout_a_in_kernel(wt_ref, g_ref, o_ref):
    o1, o2, o3 = A_Q_RANK, A_Q_RANK + A_KV_RANK, A_Q_RANK + A_KV_RANK + A_ROPE
    o4 = o3 + X_WIDTH
    g = g_ref[0]
    chunk = 512

    def put(dst, src, n, scale):
        for c in range(0, n, chunk):
            x = wt_ref[0, src + c:src + c + chunk, :] * (g * scale)
            o_ref[0, :, dst + c:dst + c + chunk] = x.T.astype(BF16)

    put(0, o4, 4096, 1.0)
    put(4096, 0, A_Q_RANK, 1.0)
    put(5120, o3, X_WIDTH, X_HEAD_DIM ** -0.5)
    put(6144, o1, A_KV_RANK, 1.0)
    half = A_ROPE // 2
    t1 = wt_ref[0, o2:o2 + half, :]
    t2 = wt_ref[0, o2 + half:o3, :]
    kr = jnp.concatenate([t1, t2, t2, t1], axis=0) * g
    o_ref[0, :, 6656:6784] = kr.T.astype(BF16)
    o_ref[0, :, 6784:] = jnp.zeros((o_ref.shape[1], A_IN_PAD - 6784), BF16)


def relayout_a_in(w, g, tk=256):
    n_layers, k, n = w.shape
    return pl.pallas_call(
        _relayout_a_in_kernel,
        grid=(n_layers, k // tk),
        in_specs=[pl.BlockSpec((1, n, tk), lambda a, i: (a, 0, i)),
                  pl.BlockSpec((1, 1, tk), lambda a, i: (a, 0, i))],
        out_specs=pl.BlockSpec((1, tk, A_IN_PAD), lambda a, i: (a, i, 0)),
        out_shape=jax.ShapeDtypeStruct((n_layers, k, A_IN_PAD), BF16),
        compiler_params=_params(("parallel", "parallel")),
        name="relayout_a_w_in",
    )(jnp.swapaxes(w, 1, 2), g.astype(F32)[:, None, :])


def _rope_tables(positions):
    inv = ROPE_THETA ** (-jnp.arange(0, A_ROPE, 2, dtype=F32) / A_ROPE)
    ang = positions.astype(F32).reshape(-1)[:, None] * inv
    cos, sin = jnp.cos(ang), jnp.sin(ang)
    zero = jnp.zeros_like(cos)
    ct = jnp.concatenate([cos, cos, zero, zero], axis=1)
    st = jnp.concatenate([-sin, sin, zero, zero], axis=1)
    return ct, st


def _swap_halves(w):
    half = w.shape[-1] // 2
    return jnp.concatenate([w[..., half:], w[..., :half]], axis=-1)


def _prep_a_w_qb(w):
    n_layers = w.shape[0]
    w = w.astype(BF16).reshape(n_layers, A_Q_RANK, A_HEADS, A_NOPE + A_ROPE)
    rope = w[..., A_NOPE:]
    out = jnp.concatenate([w[..., :A_NOPE], rope, _swap_halves(rope)], axis=-1)
    return out.reshape(n_layers, A_Q_RANK, A_HEADS * A_QPAD).swapaxes(1, 2)


def _prep_a_w_kvb(w):
    n_layers = w.shape[0]
    w = w.astype(BF16).reshape(n_layers, A_KV_RANK, A_HEADS, A_NOPE + A_VDIM)
    wk = w[..., :A_NOPE].reshape(n_layers, A_KV_RANK, A_HEADS * A_NOPE)
    wvt = w[..., A_NOPE:].reshape(n_layers, A_KV_RANK, A_HEADS * A_VDIM).swapaxes(1, 2)
    return wk, wvt


def _prep_b_w_in(w, g):
    nq = B_HEADS * B_HEAD_DIM
    nk = B_KV_HEADS * B_HEAD_DIM
    col = jnp.arange(w.shape[-1])
    scale = jnp.where(col < nq, B_HEAD_DIM ** -0.5,
                      jnp.where((col >= nq + 2 * nk) & (col < nq + 2 * nk + X_WIDTH), X_HEAD_DIM ** -0.5, 1.0))
    return (w * (g.astype(F32)[:, :, None] * scale.astype(F32))).astype(BF16)


def _t5_bucket(dist):
    n = jnp.maximum(dist, 0)
    nf = jnp.maximum(n, 1).astype(F32)
    large = MAX_EXACT + (jnp.log(nf / MAX_EXACT) / math.log(MAX_DIST / MAX_EXACT)
                         * (N_BUCKETS - MAX_EXACT)).astype(jnp.int32)
    large = jnp.minimum(large, N_BUCKETS - 1)
    return jnp.where(n < MAX_EXACT, n, large)


def _swa_tables(rel_bias, sinks):
    q_local = jnp.arange(BLOCK)[None, :]
    slot = jnp.arange(BLOCK)[:, None]
    dist = jnp.where(slot > q_local, q_local + BLOCK - slot, q_local - slot)
    onehot = (_t5_bucket(dist)[:, :, None] == jnp.arange(N_BUCKETS)).astype(F32)
    bias = jnp.einsum("kqn,nh->kqh", onehot, rel_bias.astype(F32), precision=lax.Precision.HIGHEST)
    bias = bias.reshape(BLOCK, BLOCK, B_PAIRS, B_GROUP, 2)
    bias = bias.transpose(2, 3, 0, 4, 1).reshape(B_PAIRS, B_GROUP, BLOCK, 2 * BLOCK)
    sink = sinks.astype(F32).reshape(B_PAIRS, B_GROUP, 2, 1)
    sink = jnp.broadcast_to(sink, (B_PAIRS, B_GROUP, 2, BLOCK)).reshape(B_PAIRS, B_GROUP, 1, 2 * BLOCK)
    return bias, sink


def kernel(x, mem, positions, norm_g, mem_norm_g, final_norm_g, w_mem_kv, w_out, a_w_in, a_q_norm_g, a_kv_norm_g,
           a_w_qb, a_w_kvb, b_w_in, b_sinks, rel_bias):
    batch, seq, d = x.shape
    n_mem = mem.shape[1]
    depth = norm_g.shape[0]
    xs = x.reshape(batch * seq, d)
    mems = mem.reshape(batch * n_mem, d)
    ct, st = _rope_tables(positions)
    ctt, stt = ct.T, st.T
    mkv = mem_kv_all(mems, mem_norm_g, w_mem_kv.astype(BF16))
    w_out_b = w_out.astype(BF16)
    a_w_in_b = relayout_a_in(a_w_in, norm_g[0::2])
    b_w_in_b = _prep_b_w_in(b_w_in, norm_g[1::2])
    a_w_qb_b = _prep_a_w_qb(a_w_qb)
    a_wk_b, a_wvt_b = _prep_a_w_kvb(a_w_kvb)

    xb, ss = cast_sumsq(xs)
    for i in range(depth):
        j = i // 2
        if i % 2 == 0:
            proj = norm_matmul(xb, ss, a_w_in_b, j, "mla_in_proj", tn=768)
            qt = mla_q(proj, a_q_norm_g[j], a_w_qb_b, j, ctt, stt, cq_blk=4)
            kn, vt, kr = mla_kv(proj, a_kv_norm_g[j], a_wk_b, a_wvt_b, j, ct, st, ckv_blk=12, kr_blk=52)
            y_self = mla_attention(qt, proj, kn, vt, kr, batch, seq)
            y_mem = mem_attention(proj, mkv, i, batch, seq, xq_blk=5, z_blk=3)
        else:
            proj = norm_matmul(xb, ss, b_w_in_b, j, "swa_in_proj")
            bias, sink = _swa_tables(rel_bias, b_sinks[j])
            y_self = swa_attention(proj, bias, sink, batch, seq)
            y_mem = mem_attention(proj, mkv, i, batch, seq, xq_blk=4, z_blk=8)
        xs, xb, ss = out_proj(y_self, y_mem, xs, w_out_b, i)
    out = rmsnorm_from_sumsq(xs, ss, final_norm_g)
    return out.reshape(batch, seq, d)
```
